```python
import math
import jax, jax.numpy as jnp
from jax import lax
import numpy as np

D_MODEL = 4096
BATCH = 1
SEQ = 8192
DEPTH = 2

HEAD_DIM = 128
SSM_CHANNELS = D_MODEL // 4
N_ATTN_HEADS = (D_MODEL - SSM_CHANNELS) // HEAD_DIM
N_SB_HEADS = N_ATTN_HEADS // 2
N_DIL_HEADS = N_ATTN_HEADS - N_SB_HEADS
SB_WIDTH = N_SB_HEADS * HEAD_DIM
DIL_WIDTH = N_DIL_HEADS * HEAD_DIM
MIX_WIDTH = SB_WIDTH + DIL_WIDTH + SSM_CHANNELS
SSM_GROUP = 16
N_SSM_GROUPS = SSM_CHANNELS // SSM_GROUP
SSM_STATE = 64
IN_WIDTH = 3 * SB_WIDTH + 3 * DIL_WIDTH + SSM_CHANNELS
FFN_HIDDEN = -(-8 * D_MODEL // (3 * 256)) * 256
SB_BLOCK = 128
DIL_BLOCK = 128
DIL_PATTERNS = ((128, 1), (512, 4), (2048, 16))
RMS_EPS = 1e-6

kernel_name = "hybrid_sb_dilated_s5_block"


def rms_norm(x, g):
    xf = x.astype(jnp.float32)
    y = xf * lax.rsqrt(jnp.mean(xf * xf, axis=-1, keepdims=True) + RMS_EPS)
    return (y * g.astype(jnp.float32)).astype(x.dtype)


def stick_breaking_attention(q, k, v):
    b, L, h, dh = q.shape
    nb = L // SB_BLOCK
    scale = dh ** -0.5
    kf = k.astype(jnp.float32)
    vf = v.astype(jnp.float32)
    qb = jnp.moveaxis(q.astype(jnp.float32).reshape(b, nb, SB_BLOCK, h, dh), 1, 0)
    key_pos = jnp.arange(L)

    def one_block(args):
        qi, blk = args
        q_pos = blk * SB_BLOCK + jnp.arange(SB_BLOCK)
        z = jnp.einsum('bqhd,bkhd->bhqk', qi, kf) * scale
        causal = key_pos[None, :] < q_pos[:, None]
        log_1m_beta = jnp.where(causal, -jax.nn.softplus(z), 0.0)
        suffix = lax.cumsum(log_1m_beta, axis=3, reverse=True)
        excl = jnp.concatenate([suffix[..., 1:], jnp.zeros_like(suffix[..., :1])], axis=-1)
        log_a = jax.nn.log_sigmoid(z) + excl
        a = jnp.where(causal, jnp.exp(log_a), 0.0)
        return jnp.einsum('bhqk,bkhd->bqhd', a, vf)

    out = lax.map(one_block, (qb, jnp.arange(nb)))
    return jnp.moveaxis(out, 0, 1).reshape(b, L, h, dh)


def dilated_branch(q, k, v, window, dilation):
    b, L, h, dh = q.shape
    scale = dh ** -0.5
    span = dilation * DIL_BLOCK
    L_pad = -(-L // span) * span
    Ls = L_pad // dilation
    nb = Ls // DIL_BLOCK
    n_back = window // dilation

    def to_sub(x):
        x = jnp.pad(x.astype(jnp.float32), ((0, 0), (0, L_pad - L), (0, 0), (0, 0)))
        x = jnp.moveaxis(x.reshape(b, Ls, dilation, h, dh), 2, 1)
        return x.reshape(b, dilation, nb, DIL_BLOCK, h, dh)

    def with_prev(x):
        prev = jnp.pad(x, ((0, 0), (0, 0), (1, 0), (0, 0), (0, 0), (0, 0)))[:, :, :-1]
        return jnp.concatenate([prev, x], axis=3)

    qs = to_sub(q)
    kk = with_prev(to_sub(k))
    vv = with_prev(to_sub(v))
    scores = jnp.einsum('brnqhd,brnkhd->brnhqk', qs, kk) * scale
    q_idx = jnp.arange(DIL_BLOCK)
    k_idx = jnp.arange(2 * DIL_BLOCK)
    rel = q_idx[:, None] + DIL_BLOCK - k_idx[None, :]
    band = (rel >= 0) & (rel <= n_back)
    blk = jnp.arange(nb)
    valid = band[None] & ((blk[:, None, None] > 0) | (k_idx >= DIL_BLOCK)[None, None, :])
    valid = valid[None, None, :, None]
    m = jnp.max(jnp.where(valid, scores, -jnp.inf), axis=-1)
    p = jnp.where(valid, jnp.exp(scores - m[..., None]), 0.0)
    s = jnp.sum(p, axis=-1)
    o = jnp.einsum('brnhqk,brnkhd->brnqhd', p, vv)

    def from_sub(x):
        rest = x.shape[4:]
        x = jnp.moveaxis(x.reshape((b, dilation, Ls) + rest), 1, 2)
        return x.reshape((b, L_pad) + rest)[:, :L]

    m = from_sub(jnp.swapaxes(m, 3, 4))
    s = from_sub(jnp.swapaxes(s, 3, 4))
    o = from_sub(o)
    return m, s, o


def dilated_attention(q, k, v):
    branches = [dilated_branch(q, k, v, w, d) for (w, d) in DIL_PATTERNS]
    m_all = jnp.stack([br[0] for br in branches])
    m_max = jnp.max(m_all, axis=0)
    wts = jnp.exp(m_all - m_max)
    num = sum(wts[i][..., None] * branches[i][2] for i in range(len(branches)))
    den = sum(wts[i] * branches[i][1] for i in range(len(branches)))
    return num / den[..., None]


def s5_layer(u, lam_re, lam_im, log_dt, b_re, b_im, c_re, c_im, d_skip, w_glu, b_glu):
    b, L, _ = u.shape
    uf = u.astype(jnp.float32)
    ug = uf.reshape(b, L, N_SSM_GROUPS, SSM_GROUP)
    a_re = jnp.minimum(lam_re.astype(jnp.float32), -1e-4)
    a_im = lam_im.astype(jnp.float32)
    dt = jnp.exp(log_dt.astype(jnp.float32))[:, None]
    mag = jnp.exp(dt * a_re)
    ang = dt * a_im
    abar_re = mag * jnp.cos(ang)
    abar_im = mag * jnp.sin(ang)
    den = a_re * a_re + a_im * a_im
    nr = abar_re - 1.0
    f_re = (nr * a_re + abar_im * a_im) / den
    f_im = (abar_im * a_re - nr * a_im) / den
    br = b_re.astype(jnp.float32)
    bi = b_im.astype(jnp.float32)
    bbar_re = f_re[..., None] * br - f_im[..., None] * bi
    bbar_im = f_re[..., None] * bi + f_im[..., None] * br
    bu_re = jnp.einsum('blgc,gnc->blgn', ug, bbar_re)
    bu_im = jnp.einsum('blgc,gnc->blgn', ug, bbar_im)
    ab_re = jnp.broadcast_to(abar_re, bu_re.shape)
    ab_im = jnp.broadcast_to(abar_im, bu_im.shape)

    def combine(e1, e2):
        a1r, a1i, b1r, b1i = e1
        a2r, a2i, b2r, b2i = e2
        return (a1r * a2r - a1i * a2i,
                a1r * a2i + a1i * a2r,
                a2r * b1r - a2i * b1i + b2r,
                a2r * b1i + a2i * b1r + b2i)

    _, _, x_re, x_im = lax.associative_scan(combine, (ab_re, ab_im, bu_re, bu_im), axis=1)
    y = (jnp.einsum('blgn,gcn->blgc', x_re, c_re.astype(jnp.float32))
         - jnp.einsum('blgn,gcn->blgc', x_im, c_im.astype(jnp.float32)))
    y = y.reshape(b, L, SSM_CHANNELS) + d_skip.astype(jnp.float32) * uf
    g = jax.nn.gelu(y)
    out = g * jax.nn.sigmoid(g @ w_glu.astype(jnp.float32) + b_glu.astype(jnp.float32))
    return out.astype(u.dtype)


def hybrid_layer(x, g_mix_pre, g_mix_post, g_ffn_pre, g_ffn_post, w_in,
                 g_out_sb, g_out_dil, g_out_ssm,
                 lam_re, lam_im, log_dt, b_re, b_im, c_re, c_im, d_skip, w_glu, b_glu,
                 w_out, w_gate, w_up, w_down):
    b, L, _ = x.shape
    h = rms_norm(x, g_mix_pre)
    proj = h @ w_in
    cuts = [SB_WIDTH, 2 * SB_WIDTH, 3 * SB_WIDTH,
            3 * SB_WIDTH + DIL_WIDTH, 3 * SB_WIDTH + 2 * DIL_WIDTH, 3 * SB_WIDTH + 3 * DIL_WIDTH]
    q_sb, k_sb, v_sb, q_dl, k_dl, v_dl, u_ssm = jnp.split(proj, cuts, axis=-1)
    heads_sb = lambda t: t.reshape(b, L, N_SB_HEADS, HEAD_DIM)
    heads_dl = lambda t: t.reshape(b, L, N_DIL_HEADS, HEAD_DIM)
    o_sb = stick_breaking_attention(heads_sb(q_sb), heads_sb(k_sb), heads_sb(v_sb))
    o_sb = o_sb.reshape(b, L, SB_WIDTH).astype(x.dtype)
    o_dl = dilated_attention(heads_dl(q_dl), heads_dl(k_dl), heads_dl(v_dl))
    o_dl = o_dl.reshape(b, L, DIL_WIDTH).astype(x.dtype)
    o_ssm = s5_layer(u_ssm, lam_re, lam_im, log_dt, b_re, b_im, c_re, c_im, d_skip, w_glu, b_glu)
    mixed = jnp.concatenate([rms_norm(o_sb, g_out_sb),
                             rms_norm(o_dl, g_out_dil),
                             rms_norm(o_ssm, g_out_ssm)], axis=-1)
    x = x + rms_norm(mixed @ w_out, g_mix_post)
    h = rms_norm(x, g_ffn_pre)
    f = (jax.nn.silu(h @ w_gate) * (h @ w_up)) @ w_down
    return x + rms_norm(f, g_ffn_post)


def setup_inputs(seed: int = 0) -> dict:
    key = jax.random.key(seed)
    ks = jax.random.split(key, 24)
    f32 = jnp.float32
    nrm = lambda k, shape, s: jax.random.normal(k, shape, f32) * s
    gain = lambda k, n: 1.0 + 0.02 * jax.random.normal(k, (DEPTH, n), f32)
    n_idx = jnp.arange(SSM_STATE, dtype=f32)
    lam_re = -0.5 + 0.01 * jax.random.normal(ks[9], (DEPTH, N_SSM_GROUPS, SSM_STATE), f32)
    lam_im = math.pi * n_idx[None, None, :] + 0.01 * jax.random.normal(ks[10], (DEPTH, N_SSM_GROUPS, SSM_STATE), f32)
    log_dt = jax.random.uniform(ks[11], (DEPTH, N_SSM_GROUPS), f32, math.log(1e-3), math.log(1e-1))
    return {
        "x": jax.random.normal(ks[0], (BATCH, SEQ, D_MODEL), f32),
        "norm_mix_pre": gain(ks[1], D_MODEL),
        "norm_mix_post": gain(ks[2], D_MODEL),
        "norm_ffn_pre": gain(ks[3], D_MODEL),
        "norm_ffn_post": gain(ks[4], D_MODEL),
        "w_in": nrm(ks[5], (DEPTH, D_MODEL, IN_WIDTH), D_MODEL ** -0.5),
        "norm_out_sb": gain(ks[6], SB_WIDTH),
        "norm_out_dil": gain(ks[7], DIL_WIDTH),
        "norm_out_ssm": gain(ks[8], SSM_CHANNELS),
        "ssm_lambda_re": lam_re,
        "ssm_lambda_im": lam_im,
        "ssm_log_dt": log_dt,
        "ssm_b_re": nrm(ks[12], (DEPTH, N_SSM_GROUPS, SSM_STATE, SSM_GROUP), (2.0 * SSM_GROUP) ** -0.5),
        "ssm_b_im": nrm(ks[13], (DEPTH, N_SSM_GROUPS, SSM_STATE, SSM_GROUP), (2.0 * SSM_GROUP) ** -0.5),
        "ssm_c_re": nrm(ks[14], (DEPTH, N_SSM_GROUPS, SSM_GROUP, SSM_STATE), (2.0 * SSM_STATE) ** -0.5),
        "ssm_c_im": nrm(ks[15], (DEPTH, N_SSM_GROUPS, SSM_GROUP, SSM_STATE), (2.0 * SSM_STATE) ** -0.5),
        "ssm_d": nrm(ks[16], (DEPTH, SSM_CHANNELS), 1.0),
        "ssm_w_glu": nrm(ks[17], (DEPTH, SSM_CHANNELS, SSM_CHANNELS), SSM_CHANNELS ** -0.5),
        "ssm_b_glu": nrm(ks[18], (DEPTH, SSM_CHANNELS), 0.02),
        "w_out": nrm(ks[19], (DEPTH, MIX_WIDTH, D_MODEL), MIX_WIDTH ** -0.5),
        "ffn_w_gate": nrm(ks[20], (DEPTH, D_MODEL, FFN_HIDDEN), D_MODEL ** -0.5),
        "ffn_w_up": nrm(ks[21], (DEPTH, D_MODEL, FFN_HIDDEN), D_MODEL ** -0.5),
        "ffn_w_down": nrm(ks[22], (DEPTH, FFN_HIDDEN, D_MODEL), FFN_HIDDEN ** -0.5),
    }


def reference(x, norm_mix_pre, norm_mix_post, norm_ffn_pre, norm_ffn_post, w_in,
              norm_out_sb, norm_out_dil, norm_out_ssm,
              ssm_lambda_re, ssm_lambda_im, ssm_log_dt, ssm_b_re, ssm_b_im, ssm_c_re, ssm_c_im,
              ssm_d, ssm_w_glu, ssm_b_glu, w_out, ffn_w_gate, ffn_w_up, ffn_w_down):
    for l in range(DEPTH):
        x = hybrid_layer(x, norm_mix_pre[l], norm_mix_post[l], norm_ffn_pre[l], norm_ffn_post[l], w_in[l],
                         norm_out_sb[l], norm_out_dil[l], norm_out_ssm[l],
                         ssm_lambda_re[l], ssm_lambda_im[l], ssm_log_dt[l], ssm_b_re[l], ssm_b_im[l],
                         ssm_c_re[l], ssm_c_im[l], ssm_d[l], ssm_w_glu[l], ssm_b_glu[l],
                         w_out[l], ffn_w_gate[l], ffn_w_up[l], ffn_w_down[l])
    return x
```

```python
import functools

import jax
import jax.numpy as jnp
from jax import lax
from jax.experimental import pallas as pl
from jax.experimental.pallas import tpu as pltpu

HEAD_DIM = 128
SSM_GROUP = 16
SSM_STATE = 64
DIL_BLOCK = 128
DIL_PATTERNS = ((128, 1), (512, 4), (2048, 16))
RMS_EPS = 1e-6

LANES = 128
SUBLANES = 8
GROUPS_PER_TILE = LANES // SSM_GROUP
STATES_PER_TILE = GROUPS_PER_TILE * SSM_STATE
STAT_S_LANE = 16
V7X_VMEM_LIMIT_BYTES = 56 * 1024 * 1024

F32 = jnp.float32
BF16 = jnp.bfloat16


def _pick(n, candidates):
    for c in candidates:
        if n % c == 0:
            return c
    raise ValueError(f"no tile in {candidates} divides {n}")


def _params(*sem):
    return pltpu.CompilerParams(dimension_semantics=sem, vmem_limit_bytes=V7X_VMEM_LIMIT_BYTES)


def _rms(x, g):
    return x * lax.rsqrt(jnp.mean(x * x, axis=-1, keepdims=True) + RMS_EPS) * g


def _dot(a, b):
    return jnp.dot(a, b, preferred_element_type=F32)


def _dot_nt(a, b):
    return lax.dot_general(a, b, (((1,), (1,)), ((), ())), preferred_element_type=F32)


def _norm_mm_kernel(x_ref, g_ref, w_ref, o_ref, h_ref):
    @pl.when(pl.program_id(1) == 0)
    def _():
        h_ref[...] = _rms(x_ref[...], g_ref[...]).astype(h_ref.dtype)

    o_ref[...] = _dot(h_ref[...], w_ref[...]).astype(o_ref.dtype)


def _norm_mm(x, g, w, layer, col0, ncols, out_dtype):
    L, D = x.shape
    tm = _pick(L, (512, 256, 128))
    tn = _pick(ncols, (1024, 768, 512, 256, 128))
    assert col0 % tn == 0
    jb = col0 // tn
    return pl.pallas_call(
        _norm_mm_kernel,
        grid=(L // tm, ncols // tn),
        in_specs=[
            pl.BlockSpec((tm, D), lambda i, j: (i, 0)),
            pl.BlockSpec((1, D), lambda i, j: (0, 0)),
            pl.BlockSpec((None, D, tn), lambda i, j: (layer, 0, j + jb)),
        ],
        out_specs=pl.BlockSpec((tm, tn), lambda i, j: (i, j)),
        out_shape=jax.ShapeDtypeStruct((L, ncols), out_dtype),
        scratch_shapes=[pltpu.VMEM((tm, D), BF16)],
        compiler_params=_params("parallel", "arbitrary"),
        name="norm_mm",
    )(x, g, w)


def _ffn_in_kernel(x_ref, g_ref, wg_ref, wu_ref, o_ref, h_ref):
    @pl.when(pl.program_id(1) == 0)
    def _():
        h_ref[...] = _rms(x_ref[...], g_ref[...]).astype(h_ref.dtype)

    h = h_ref[...]
    gate = _dot(h, wg_ref[...])
    up = _dot(h, wu_ref[...])
    o_ref[...] = (jax.nn.silu(gate) * up).astype(o_ref.dtype)


def _ffn_in(x, g, wg, wu, layer):
    L, D = x.shape
    F = wg.shape[-1]
    tm = _pick(L, (512, 256, 128))
    tn = _pick(F, (512, 256, 128))
    wspec = pl.BlockSpec((None, D, tn), lambda i, j: (layer, 0, j))
    return pl.pallas_call(
        _ffn_in_kernel,
        grid=(L // tm, F // tn),
        in_specs=[
            pl.BlockSpec((tm, D), lambda i, j: (i, 0)),
            pl.BlockSpec((1, D), lambda i, j: (0, 0)),
            wspec, wspec,
        ],
        out_specs=pl.BlockSpec((tm, tn), lambda i, j: (i, j)),
        out_shape=jax.ShapeDtypeStruct((L, F), BF16),
        scratch_shapes=[pltpu.VMEM((tm, D), BF16)],
        compiler_params=_params("parallel", "arbitrary"),
        name="ffn_in",
    )(x, g, wg, wu)


def _mm_kernel(a_ref, w_ref, o_ref):
    o_ref[...] = _dot(a_ref[...], w_ref[...]).astype(o_ref.dtype)


def _mm(a, w, layer):
    L, K = a.shape
    N = w.shape[-1]
    tm = _pick(L, (512, 256, 128))
    tn = _pick(N, (512, 256, 128))
    return pl.pallas_call(
        _mm_kernel,
        grid=(L // tm, N // tn),
        in_specs=[
            pl.BlockSpec((tm, K), lambda i, j: (i, 0)),
            pl.BlockSpec((None, K, tn), lambda i, j: (layer, 0, j)),
        ],
        out_specs=pl.BlockSpec((tm, tn), lambda i, j: (i, j)),
        out_shape=jax.ShapeDtypeStruct((L, N), F32),
        compiler_params=_params("parallel", "arbitrary"),
        name="mm",
    )(a, w)


def _resid_norm_kernel(x_ref, y_ref, g_ref, o_ref):
    o_ref[...] = x_ref[...] + _rms(y_ref[...], g_ref[...])


def _resid_norm(x, y, g):
    L, D = x.shape
    tm = _pick(L, (256, 128))
    row = pl.BlockSpec((tm, D), lambda i: (i, 0))
    return pl.pallas_call(
        _resid_norm_kernel,
        grid=(L // tm,),
        in_specs=[row, row, pl.BlockSpec((1, D), lambda i: (0, 0))],
        out_specs=row,
        out_shape=jax.ShapeDtypeStruct((L, D), F32),
        compiler_params=_params("parallel"),
        name="resid_norm",
    )(x, y, g)


def _out_proj_kernel(a_ref, b_ref, c_ref, ga_ref, gb_ref, gc_ref, w_ref, o_ref, h_ref):
    @pl.when(pl.program_id(1) == 0)
    def _():
        wa = a_ref.shape[1]
        wb = b_ref.shape[1]
        h_ref[:, :wa] = _rms(a_ref[...], ga_ref[...]).astype(h_ref.dtype)
        h_ref[:, wa:wa + wb] = _rms(b_ref[...], gb_ref[...]).astype(h_ref.dtype)
        h_ref[:, wa + wb:] = _rms(c_ref[...], gc_ref[...]).astype(h_ref.dtype)

    o_ref[...] = _dot(h_ref[...], w_ref[...])


def _out_proj(a, b, c, ga, gb, gc, w, layer):
    L = a.shape[0]
    K = a.shape[1] + b.shape[1] + c.shape[1]
    N = w.shape[-1]
    tm = _pick(L, (512, 256, 128))
    tn = _pick(N, (1024, 512, 256, 128))
    rows = lambda arr: pl.BlockSpec((tm, arr.shape[1]), lambda i, j: (i, 0))
    gain = lambda arr: pl.BlockSpec((1, arr.shape[1]), lambda i, j: (0, 0))
    return pl.pallas_call(
        _out_proj_kernel,
        grid=(L // tm, N // tn),
        in_specs=[rows(a), rows(b), rows(c), gain(ga), gain(gb), gain(gc),
                  pl.BlockSpec((None, K, tn), lambda i, j: (layer, 0, j))],
        out_specs=pl.BlockSpec((tm, tn), lambda i, j: (i, j)),
        out_shape=jax.ShapeDtypeStruct((L, N), F32),
        scratch_shapes=[pltpu.VMEM((tm, K), BF16)],
        compiler_params=_params("parallel", "arbitrary"),
        name="out_proj",
    )(a, b, c, ga, gb, gc, w)


def _sb_kernel(q_ref, k_ref, v_ref, o_ref, *, tq, scale):
    i = pl.program_id(1)
    q = q_ref[...]
    row = lax.broadcasted_iota(jnp.int32, (tq, tq), 0)
    col = lax.broadcasted_iota(jnp.int32, (tq, tq), 1)
    after = jnp.where(row > col, 1.0, 0.0).astype(BF16)
    causal = col < row

    def block(off, carry, diagonal):
        kb = k_ref[pl.ds(off, tq), :]
        vb = v_ref[pl.ds(off, tq), :]
        z = _dot_nt(q, kb) * scale
        t = jnp.log1p(jnp.exp(-jnp.abs(z)))
        log_beta = jnp.minimum(z, 0.0) - t
        l1m = -jnp.maximum(z, 0.0) - t
        if diagonal:
            l1m = jnp.where(causal, l1m, 0.0)
        hi = l1m.astype(BF16)
        lo = (l1m - hi.astype(F32)).astype(BF16)
        suffix = _dot(hi, after) + _dot(lo, after)
        a = jnp.exp(log_beta + suffix + carry)
        if diagonal:
            a = jnp.where(causal, a, 0.0)
        pv = _dot(a.astype(BF16), vb)
        return pv, carry + jnp.sum(l1m, axis=-1, keepdims=True)

    acc, carry = block(pl.multiple_of(i * tq, tq), jnp.zeros((tq, 1), F32), True)

    def body(s, c):
        acc, carry = c
        pv, carry = block(pl.multiple_of((i - 1 - s) * tq, tq), carry, False)
        return acc + pv, carry

    acc, _ = lax.fori_loop(0, i, body, (acc, carry))
    o_ref[...] = acc


def _sb_attention(qkv, n_heads):
    L = qkv.shape[0]
    tq = _pick(L, (256, 128))
    kern = functools.partial(_sb_kernel, tq=tq, scale=HEAD_DIM ** -0.5)
    return pl.pallas_call(
        kern,
        grid=(n_heads, L // tq),
        in_specs=[
            pl.BlockSpec((tq, HEAD_DIM), lambda h, i: (i, h)),
            pl.BlockSpec((L, HEAD_DIM), lambda h, i: (0, n_heads + h)),
            pl.BlockSpec((L, HEAD_DIM), lambda h, i: (0, 2 * n_heads + h)),
        ],
        out_specs=pl.BlockSpec((tq, HEAD_DIM), lambda h, i: (i, h)),
        out_shape=jax.ShapeDtypeStruct((L, n_heads * HEAD_DIM), F32),
        compiler_params=_params("parallel", "arbitrary"),
        name="sb_attention",
    )(qkv, qkv, qkv)


def _dil_branch_kernel(q_ref, kc_ref, vc_ref, kp_ref, vp_ref, o_ref, st_ref, *, n_heads, scale):
    has_prev = pl.program_id(1) > 0
    qi = lax.broadcasted_iota(jnp.int32, (DIL_BLOCK, DIL_BLOCK), 0)
    ki = lax.broadcasted_iota(jnp.int32, (DIL_BLOCK, DIL_BLOCK), 1)
    mask_prev = (ki >= qi) & has_prev
    mask_cur = ki <= qi
    stats = jnp.zeros((DIL_BLOCK, LANES), F32)
    for h in range(n_heads):
        sl = slice(h * HEAD_DIM, (h + 1) * HEAD_DIM)
        q = q_ref[:, sl]
        sp = jnp.where(mask_prev, _dot_nt(q, kp_ref[:, sl]) * scale, -jnp.inf)
        sc = jnp.where(mask_cur, _dot_nt(q, kc_ref[:, sl]) * scale, -jnp.inf)
        m = jnp.maximum(jnp.max(sp, axis=-1, keepdims=True), jnp.max(sc, axis=-1, keepdims=True))
        pp = jnp.exp(sp - m)
        pc = jnp.exp(sc - m)
        s = jnp.sum(pp, axis=-1, keepdims=True) + jnp.sum(pc, axis=-1, keepdims=True)
        o_ref[:, sl] = _dot(pp.astype(BF16), vp_ref[:, sl]) + _dot(pc.astype(BF16), vc_ref[:, sl])
        stats = jnp.where(ki == h, m, stats)
        stats = jnp.where(ki == STAT_S_LANE + h, s, stats)
    st_ref[...] = stats


def _dil_branch(qkv, n_heads, dilation):
    L, C = qkv.shape
    W = n_heads * HEAD_DIM
    Ls = L // dilation
    nb = Ls // DIL_BLOCK
    view = qkv.reshape(Ls, dilation * C)
    per_row = C // W
    spec = lambda which, prev: pl.BlockSpec(
        (DIL_BLOCK, W),
        (lambda r, n: (jnp.maximum(n - 1, 0), r * per_row + which)) if prev
        else (lambda r, n: (n, r * per_row + which)))
    kern = functools.partial(_dil_branch_kernel, n_heads=n_heads, scale=HEAD_DIM ** -0.5)
    o, st = pl.pallas_call(
        kern,
        grid=(dilation, nb),
        in_specs=[spec(3, False), spec(4, False), spec(5, False), spec(4, True), spec(5, True)],
        out_specs=[pl.BlockSpec((DIL_BLOCK, W), lambda r, n: (n, r)),
                   pl.BlockSpec((DIL_BLOCK, LANES), lambda r, n: (n, r))],
        out_shape=[jax.ShapeDtypeStruct((Ls, dilation * W), F32),
                   jax.ShapeDtypeStruct((Ls, dilation * LANES), F32)],
        compiler_params=_params("parallel", "arbitrary"),
        name=f"dilated_d{dilation}",
    )(view, view, view, view, view)
    return o.reshape(L, W), st.reshape(L, LANES)


def _dil_combine_kernel(*refs, n_heads, n_br):
    o_refs = refs[:n_br]
    st_refs = refs[n_br:2 * n_br]
    out_ref = refs[2 * n_br]
    stats = [r[...] for r in st_refs]
    for h in range(n_heads):
        sl = slice(h * HEAD_DIM, (h + 1) * HEAD_DIM)
        ms = [st[:, h:h + 1] for st in stats]
        ss = [st[:, STAT_S_LANE + h:STAT_S_LANE + h + 1] for st in stats]
        m_max = functools.reduce(jnp.maximum, ms)
        ws = [jnp.exp(m - m_max) for m in ms]
        num = sum(w * r[:, sl] for w, r in zip(ws, o_refs))
        den = sum(w * s for w, s in zip(ws, ss))
        out_ref[:, sl] = num / den


def _dil_attention(qkv, n_heads):
    assert n_heads <= STAT_S_LANE and STAT_S_LANE + n_heads <= LANES
    L = qkv.shape[0]
    W = n_heads * HEAD_DIM
    outs = []
    for window, dilation in DIL_PATTERNS:
        assert window // dilation == DIL_BLOCK and L % (dilation * DIL_BLOCK) == 0
        outs.append(_dil_branch(qkv, n_heads, dilation))
    n_br = len(outs)
    tm = _pick(L, (512, 256, 128))
    kern = functools.partial(_dil_combine_kernel, n_heads=n_heads, n_br=n_br)
    return pl.pallas_call(
        kern,
        grid=(L // tm,),
        in_specs=[pl.BlockSpec((tm, W), lambda i: (i, 0))] * n_br
                 + [pl.BlockSpec((tm, LANES), lambda i: (i, 0))] * n_br,
        out_specs=pl.BlockSpec((tm, W), lambda i: (i, 0)),
        out_shape=jax.ShapeDtypeStruct((L, W), F32),
        compiler_params=_params("parallel"),
        name="dilated_combine",
    )(*[o for o, _ in outs], *[st for _, st in outs])


def _s5_prep_kernel(lre_ref, lim_ref, ldt_ref, bre_ref, bim_ref, pre_ref, pim_ref, bbre_ref, bbim_ref):
    a_re = jnp.minimum(lre_ref[...], -1e-4)
    a_im = lim_ref[...]
    dt = jnp.exp(ldt_ref[...])
    mag = jnp.exp(dt * a_re)
    ang = dt * a_im
    abar_re = mag * jnp.cos(ang)
    abar_im = mag * jnp.sin(ang)
    den = a_re * a_re + a_im * a_im
    nr = abar_re - 1.0
    f_re = (nr * a_re + abar_im * a_im) / den
    f_im = (abar_im * a_re - nr * a_im) / den
    for c in range(bre_ref.shape[0]):
        br = bre_ref[c]
        bi = bim_ref[c]
        bbre_ref[c] = f_re * br - f_im * bi
        bbim_ref[c] = f_re * bi + f_im * br
    pr, pi = abar_re, abar_im
    pre_ref[0] = pr
    pim_ref[0] = pi
    for k in range(1, SUBLANES):
        pr, pi = pr * abar_re - pi * abar_im, pr * abar_im + pi * abar_re
        pre_ref[k] = pr
        pim_ref[k] = pi


def _s5_scan_kernel(u_ref, wb_ref, wc_ref, pre_ref, pim_ref, d_ref, y_ref, x_ref, carry_ref):
    S = STATES_PER_TILE
    tc = u_ref.shape[0]

    @pl.when(pl.program_id(1) == 0)
    def _():
        carry_ref[...] = jnp.zeros_like(carry_ref)

    u = u_ref[...]
    x_ref[...] = _dot(u.astype(BF16), wb_ref[...])

    p_re = pre_ref[...]
    p_im = pim_ref[...]
    sub = lax.broadcasted_iota(jnp.int32, (SUBLANES, S), 0)
    steps = []
    for shift in (1, 2, 4):
        a_r = jnp.where(sub >= shift, p_re[shift - 1:shift, :], 0.0)
        a_i = jnp.where(sub >= shift, p_im[shift - 1:shift, :], 0.0)
        steps.append((shift, a_r, a_i))

    def body(b, carry):
        c_re, c_im = carry
        r0 = pl.multiple_of(b * SUBLANES, SUBLANES)
        xr = x_ref[pl.ds(r0, SUBLANES), :S]
        xi = x_ref[pl.ds(r0, SUBLANES), S:]
        for shift, a_r, a_i in steps:
            sr = pltpu.roll(xr, shift, axis=0)
            si = pltpu.roll(xi, shift, axis=0)
            xr, xi = xr + a_r * sr - a_i * si, xi + a_r * si + a_i * sr
        xr, xi = xr + p_re * c_re - p_im * c_im, xi + p_re * c_im + p_im * c_re
        x_ref[pl.ds(r0, SUBLANES), :S] = xr
        x_ref[pl.ds(r0, SUBLANES), S:] = xi
        return xr[SUBLANES - 1:, :], xi[SUBLANES - 1:, :]

    c_re, c_im = lax.fori_loop(0, tc // SUBLANES, body, (carry_ref[0:1, :S], carry_ref[0:1, S:]))
    carry_ref[0:1, :S] = c_re
    carry_ref[0:1, S:] = c_im

    y_ref[...] = _dot(x_ref[...].astype(BF16), wc_ref[...]) + d_ref[...] * u


def _s5_glu_kernel(y_ref, w_ref, b_ref, o_ref):
    g = jax.nn.gelu(y_ref[...])
    o_ref[...] = g * jax.nn.sigmoid(_dot(g.astype(BF16), w_ref[...]) + b_ref[...])


def _block_diag_tiles(m):
    G, a, b = m.shape
    T = G // GROUPS_PER_TILE
    m = m.reshape(T, GROUPS_PER_TILE, a, b)
    eye = jnp.eye(GROUPS_PER_TILE, dtype=m.dtype)
    return jnp.einsum("tgab,gh->tgahb", m, eye).reshape(T, GROUPS_PER_TILE * a, GROUPS_PER_TILE * b)


def _s5_layer(u, lam_re, lam_im, log_dt, b_re, b_im, c_re, c_im, d_skip, w_glu, b_glu, layer):
    L, C = u.shape
    G, N = lam_re.shape
    T = G // GROUPS_PER_TILE
    S = STATES_PER_TILE
    assert N == SSM_STATE and G * SSM_GROUP == C and C % LANES == 0

    full = lambda shape: pl.BlockSpec(shape, lambda: (0,) * len(shape))
    p_re, p_im, bb_re, bb_im = pl.pallas_call(
        _s5_prep_kernel,
        in_specs=[full((G, N)), full((G, N)), full((G, N)), full((SSM_GROUP, G, N)), full((SSM_GROUP, G, N))],
        out_specs=[full((SUBLANES, G, N)), full((SUBLANES, G, N)),
                   full((SSM_GROUP, G, N)), full((SSM_GROUP, G, N))],
        out_shape=[jax.ShapeDtypeStruct((SUBLANES, G, N), F32)] * 2
                  + [jax.ShapeDtypeStruct((SSM_GROUP, G, N), F32)] * 2,
        name="s5_prep",
    )(lam_re, lam_im, jnp.broadcast_to(log_dt[:, None], (G, N)),
      jnp.transpose(b_re, (2, 0, 1)), jnp.transpose(b_im, (2, 0, 1)))

    bb = lambda t: _block_diag_tiles(jnp.transpose(t, (1, 0, 2)))
    wb = jnp.concatenate([bb(bb_re), bb(bb_im)], axis=-1).astype(BF16)
    cc = lambda t: _block_diag_tiles(jnp.transpose(t, (0, 2, 1)))
    wc = jnp.concatenate([cc(c_re), -cc(c_im)], axis=1).astype(BF16)
    pw = lambda t: jnp.transpose(t.reshape(SUBLANES, T, S), (1, 0, 2))

    tc = _pick(L, (1024, 512, 256, 128))
    y = pl.pallas_call(
        _s5_scan_kernel,
        grid=(T, L // tc),
        in_specs=[
            pl.BlockSpec((tc, LANES), lambda c, t: (t, c)),
            pl.BlockSpec((None, LANES, 2 * S), lambda c, t: (c, 0, 0)),
            pl.BlockSpec((None, 2 * S, LANES), lambda c, t: (c, 0, 0)),
            pl.BlockSpec((None, SUBLANES, S), lambda c, t: (c, 0, 0)),
            pl.BlockSpec((None, SUBLANES, S), lambda c, t: (c, 0, 0)),
            pl.BlockSpec((1, LANES), lambda c, t: (0, c)),
        ],
        out_specs=pl.BlockSpec((tc, LANES), lambda c, t: (t, c)),
        out_shape=jax.ShapeDtypeStruct((L, C), F32),
        scratch_shapes=[pltpu.VMEM((tc, 2 * S), F32), pltpu.VMEM((SUBLANES, 2 * S), F32)],
        compiler_params=_params("parallel", "arbitrary"),
        name="s5_scan",
    )(u, wb, wc, pw(p_re), pw(p_im), d_skip)

    tm = _pick(L, (512, 256, 128))
    return pl.pallas_call(
        _s5_glu_kernel,
        grid=(L // tm,),
        in_specs=[pl.BlockSpec((tm, C), lambda i: (i, 0)),
                  pl.BlockSpec((None, C, C), lambda i: (layer, 0, 0)),
                  pl.BlockSpec((1, C), lambda i: (0, 0))],
        out_specs=pl.BlockSpec((tm, C), lambda i: (i, 0)),
        out_shape=jax.ShapeDtypeStruct((L, C), F32),
        compiler_params=_params("parallel"),
        name="s5_glu",
    )(y, w_glu, b_glu)


def kernel(x, norm_mix_pre, norm_mix_post, norm_ffn_pre, norm_ffn_post, w_in, norm_out_sb, norm_out_dil, norm_out_ssm, ssm_lambda_re, ssm_lambda_im, ssm_log_dt, ssm_b_re, ssm_b_im, ssm_c_re, ssm_c_im, ssm_d, ssm_w_glu, ssm_b_glu, w_out, ffn_w_gate, ffn_w_up, ffn_w_down):
    B, L, D = x.shape
    depth = w_in.shape[0]
    ssm_c = ssm_d.shape[-1]
    sb_w = norm_out_sb.shape[-1]
    dl_w = norm_out_dil.shape[-1]
    n_sb = sb_w // HEAD_DIM
    n_dl = dl_w // HEAD_DIM
    assert n_sb == n_dl and w_in.shape[-1] == 3 * sb_w + 3 * dl_w + ssm_c
    qkv_w = 3 * sb_w + 3 * dl_w

    w_in_b = w_in.astype(BF16)
    w_out_b = w_out.astype(BF16)
    w_gate_b = ffn_w_gate.astype(BF16)
    w_up_b = ffn_w_up.astype(BF16)
    w_down_b = ffn_w_down.astype(BF16)
    w_glu_b = ssm_w_glu.astype(BF16)
    row = lambda a, l: a[l][None, :]

    outs = []
    for b in range(B):
        xb = x[b]
        for l in range(depth):
            g_pre = row(norm_mix_pre, l)
            qkv = _norm_mm(xb, g_pre, w_in_b, l, 0, qkv_w, BF16)
            u = _norm_mm(xb, g_pre, w_in_b, l, qkv_w, ssm_c, F32)
            o_sb = _sb_attention(qkv, n_sb)
            o_dl = _dil_attention(qkv, n_dl)
            o_ssm = _s5_layer(u, ssm_lambda_re[l], ssm_lambda_im[l], ssm_log_dt[l], ssm_b_re[l], ssm_b_im[l],
                              ssm_c_re[l], ssm_c_im[l], row(ssm_d, l), w_glu_b, row(ssm_b_glu, l), l)
            y = _out_proj(o_sb, o_dl, o_ssm, row(norm_out_sb, l), row(norm_out_dil, l), row(norm_out_ssm, l),
                          w_out_b, l)
            xb = _resid_norm(xb, y, row(norm_mix_post, l))
            f = _ffn_in(xb, row(norm_ffn_pre, l), w_gate_b, w_up_b, l)
            y = _mm(f, w_down_b, l)
            xb = _resid_norm(xb, y, row(norm_ffn_post, l))
        outs.append(xb)
    return jnp.stack(outs)
```

```python
import functools

import jax
import jax.numpy as jnp
from jax import lax
from jax.experimental import pallas as pl
from jax.experimental.pallas import tpu as pltpu

HEAD_DIM = 128
SSM_GROUP = 16
SSM_STATE = 64
DIL_BLOCK = 128
DIL_PATTERNS = ((128, 1), (512, 4), (2048, 16))
RMS_EPS = 1e-6

LANES = 128
SUBLANES = 8
GROUPS_PER_TILE = LANES // SSM_GROUP
STATES_PER_TILE = GROUPS_PER_TILE * SSM_STATE
V7X_VMEM_LIMIT_BYTES = 56 * 1024 * 1024

SB_UNDERFLOW = 110.0
SB_HEADS_PER_STEP = 2
DIL_SUPER = DIL_BLOCK * max(d for _, d in DIL_PATTERNS)
DIL_GROUP = 4

F32 = jnp.float32
BF16 = jnp.bfloat16


def _pick(n, candidates):
    for c in candidates:
        if n % c == 0:
            return c
    raise ValueError(f"no tile in {candidates} divides {n}")


def _params(*sem):
    return pltpu.CompilerParams(dimension_semantics=sem, vmem_limit_bytes=V7X_VMEM_LIMIT_BYTES)


def _rms(x, g):
    return x * lax.rsqrt(jnp.mean(x * x, axis=-1, keepdims=True) + RMS_EPS) * g


def _dot(a, b):
    return jnp.dot(a, b, preferred_element_type=F32)


def _dot_nt(a, b):
    return lax.dot_general(a, b, (((1,), (1,)), ((), ())), preferred_element_type=F32)


def _norm_kernel(x_ref, g_ref, h_ref):
    h_ref[...] = _rms(x_ref[...], g_ref[...]).astype(h_ref.dtype)


def _norm(x, g):
    L, D = x.shape
    tm = _pick(L, (256, 128))
    row = pl.BlockSpec((tm, D), lambda i: (i, 0))
    return pl.pallas_call(
        _norm_kernel,
        grid=(L // tm,),
        in_specs=[row, pl.BlockSpec((1, D), lambda i: (0, 0))],
        out_specs=row,
        out_shape=jax.ShapeDtypeStruct((L, D), BF16),
        compiler_params=_params("parallel"),
        name="norm",
    )(x, g)


def _resid_norm_kernel(x_ref, y_ref, g_ref, o_ref):
    o_ref[...] = x_ref[...] + _rms(y_ref[...], g_ref[...])


def _resid_norm_next_kernel(x_ref, y_ref, g_ref, gn_ref, o_ref, h_ref):
    x = x_ref[...] + _rms(y_ref[...], g_ref[...])
    o_ref[...] = x
    h_ref[...] = _rms(x, gn_ref[...]).astype(h_ref.dtype)


def _resid_norm(x, y, g, g_next=None):
    L, D = x.shape
    tm = _pick(L, (256, 128))
    row = pl.BlockSpec((tm, D), lambda i: (i, 0))
    gain = pl.BlockSpec((1, D), lambda i: (0, 0))
    if g_next is None:
        return pl.pallas_call(
            _resid_norm_kernel,
            grid=(L // tm,),
            in_specs=[row, row, gain],
            out_specs=row,
            out_shape=jax.ShapeDtypeStruct((L, D), F32),
            compiler_params=_params("parallel"),
            name="resid_norm",
        )(x, y, g)
    return pl.pallas_call(
        _resid_norm_next_kernel,
        grid=(L // tm,),
        in_specs=[row, row, gain, gain],
        out_specs=[row, row],
        out_shape=[jax.ShapeDtypeStruct((L, D), F32), jax.ShapeDtypeStruct((L, D), BF16)],
        compiler_params=_params("parallel"),
        name="resid_norm_next",
    )(x, y, g, g_next)


def _mm_kernel(a_ref, w_ref, o_ref):
    o_ref[...] = _dot(a_ref[...], w_ref[...]).astype(o_ref.dtype)


def _mm(a, w, layer, col0, ncols, out_dtype, tms, tns):
    L, K = a.shape
    tm = _pick(L, tms)
    tn = _pick(ncols, tns)
    assert col0 % tn == 0
    jb = col0 // tn
    return pl.pallas_call(
        _mm_kernel,
        grid=(L // tm, ncols // tn),
        in_specs=[
            pl.BlockSpec((tm, K), lambda i, j: (i, 0)),
            pl.BlockSpec((None, K, tn), lambda i, j: (layer, 0, j + jb)),
        ],
        out_specs=pl.BlockSpec((tm, tn), lambda i, j: (i, j)),
        out_shape=jax.ShapeDtypeStruct((L, ncols), out_dtype),
        compiler_params=_params("parallel", "arbitrary"),
        name="mm",
    )(a, w)


def _ffn_in_kernel(h_ref, wg_ref, wu_ref, o_ref):
    h = h_ref[...]
    gate = _dot(h, wg_ref[...])
    up = _dot(h, wu_ref[...])
    o_ref[...] = (jax.nn.silu(gate) * up).astype(o_ref.dtype)


def _ffn_in(h, wg, wu, layer):
    L, D = h.shape
    F = wg.shape[-1]
    tm = _pick(L, (1024, 512, 256, 128))
    tn = _pick(F, (256, 128))
    wspec = pl.BlockSpec((None, D, tn), lambda i, j: (layer, 0, j))
    return pl.pallas_call(
        _ffn_in_kernel,
        grid=(L // tm, F // tn),
        in_specs=[pl.BlockSpec((tm, D), lambda i, j: (i, 0)), wspec, wspec],
        out_specs=pl.BlockSpec((tm, tn), lambda i, j: (i, j)),
        out_shape=jax.ShapeDtypeStruct((L, F), BF16),
        compiler_params=_params("parallel", "arbitrary"),
        name="ffn_in",
    )(h, wg, wu)


def _out_proj_kernel(a_ref, b_ref, c_ref, ga_ref, gb_ref, gc_ref, w_ref, o_ref, h_ref):
    @pl.when(pl.program_id(1) == 0)
    def _():
        wa = a_ref.shape[1]
        wb = b_ref.shape[1]
        h_ref[:, :wa] = _rms(a_ref[...], ga_ref[...]).astype(h_ref.dtype)
        h_ref[:, wa:wa + wb] = _rms(b_ref[...], gb_ref[...]).astype(h_ref.dtype)
        h_ref[:, wa + wb:] = _rms(c_ref[...], gc_ref[...]).astype(h_ref.dtype)

    o_ref[...] = _dot(h_ref[...], w_ref[...])


def _out_proj(a, b, c, ga, gb, gc, w, layer):
    L = a.shape[0]
    K = a.shape[1] + b.shape[1] + c.shape[1]
    N = w.shape[-1]
    tm = _pick(L, (512, 256, 128))
    tn = _pick(N, (1024, 512, 256, 128))
    rows = lambda arr: pl.BlockSpec((tm, arr.shape[1]), lambda i, j: (i, 0))
    gain = lambda arr: pl.BlockSpec((1, arr.shape[1]), lambda i, j: (0, 0))
    return pl.pallas_call(
        _out_proj_kernel,
        grid=(L // tm, N // tn),
        in_specs=[rows(a), rows(b), rows(c), gain(ga), gain(gb), gain(gc),
                  pl.BlockSpec((None, K, tn), lambda i, j: (layer, 0, j))],
        out_specs=pl.BlockSpec((tm, tn), lambda i, j: (i, j)),
        out_shape=jax.ShapeDtypeStruct((L, N), F32),
        scratch_shapes=[pltpu.VMEM((tm, K), BF16)],
        compiler_params=_params("parallel", "arbitrary"),
        name="out_proj",
    )(a, b, c, ga, gb, gc, w)


def _sb_kernel(q_ref, k_ref, v_ref, o_ref, *, tq, hp, scale):
    i = pl.program_id(1)
    heads = range(hp)
    cols = lambda h: slice(h * HEAD_DIM, (h + 1) * HEAD_DIM)
    qs = [q_ref[:, cols(h)] for h in heads]
    row = lax.broadcasted_iota(jnp.int32, (tq, tq), 0)
    col = lax.broadcasted_iota(jnp.int32, (tq, tq), 1)
    after = jnp.where(row > col, 1.0, 0.0).astype(BF16)
    after = jnp.concatenate([after, after], axis=0)
    causal = col < row

    def blocks(off, carries, diagonal):
        zs = [_dot_nt(qs[h], k_ref[pl.ds(off, tq), cols(h)]) * scale for h in heads]
        log_betas, l1ms, splits = [], [], []
        for z in zs:
            t = jnp.log1p(jnp.exp(-jnp.abs(z)))
            log_betas.append(jnp.minimum(z, 0.0) - t)
            l1m = -jnp.maximum(z, 0.0) - t
            if diagonal:
                l1m = jnp.where(causal, l1m, 0.0)
            hi = l1m.astype(BF16)
            lo = (l1m - hi.astype(F32)).astype(BF16)
            l1ms.append(l1m)
            splits.append(jnp.concatenate([hi, lo], axis=1))
        suffixes = [_dot(s, after) for s in splits]
        pvs = []
        for h in heads:
            a = jnp.exp(log_betas[h] + suffixes[h] + carries[h])
            if diagonal:
                a = jnp.where(causal, a, 0.0)
            pvs.append(_dot(a.astype(BF16), v_ref[pl.ds(off, tq), cols(h)]))
        carries = [carries[h] + jnp.sum(l1ms[h], axis=-1, keepdims=True) for h in heads]
        return pvs, carries

    def live(carries):
        return jnp.max(jnp.broadcast_to(functools.reduce(jnp.maximum, carries), (tq, LANES)))

    accs, carries = blocks(pl.multiple_of(i * tq, tq), [jnp.zeros((tq, 1), F32)] * hp, True)

    def cond(st):
        return jnp.logical_and(st[0] < i, st[1] > -SB_UNDERFLOW)

    def body(st):
        s, _, accs, carries = st
        pvs, carries = blocks(pl.multiple_of((i - 1 - s) * tq, tq), carries, False)
        return s + 1, live(carries), [a + p for a, p in zip(accs, pvs)], carries

    _, _, accs, _ = lax.while_loop(cond, body, (jnp.int32(0), live(carries), accs, carries))
    for h in heads:
        o_ref[:, cols(h)] = accs[h]


def _sb_attention(qkv, n_heads):
    L = qkv.shape[0]
    tq = _pick(L, (256, 128))
    hp = _pick(n_heads, (SB_HEADS_PER_STEP, 1))
    w = hp * HEAD_DIM
    nb = n_heads // hp
    kern = functools.partial(_sb_kernel, tq=tq, hp=hp, scale=HEAD_DIM ** -0.5)
    return pl.pallas_call(
        kern,
        grid=(nb, L // tq),
        in_specs=[
            pl.BlockSpec((tq, w), lambda h, i: (i, h)),
            pl.BlockSpec((L, w), lambda h, i: (0, nb + h)),
            pl.BlockSpec((L, w), lambda h, i: (0, 2 * nb + h)),
        ],
        out_specs=pl.BlockSpec((tq, w), lambda h, i: (i, h)),
        out_shape=jax.ShapeDtypeStruct((L, n_heads * HEAD_DIM), F32),
        compiler_params=_params("parallel", "arbitrary"),
        name="sb_attention",
    )(qkv, qkv, qkv)


def _dil_kernel(q_ref, kc_ref, vc_ref, kp_ref, vp_ref, o_ref, kcat, vcat, ob, mb, sb, *, scale):
    S = DIL_SUPER
    B = DIL_BLOCK
    first = pl.program_id(1) == 0
    kcat[:S] = kp_ref[...]
    kcat[S:] = kc_ref[...]
    vcat[:S] = vp_ref[...]
    vcat[S:] = vc_ref[...]
    qi = lax.broadcasted_iota(jnp.int32, (B, B), 0)
    ki = lax.broadcasted_iota(jnp.int32, (B, B), 1)
    in_cur = ki <= qi

    for bi, (window, d) in enumerate(DIL_PATTERNS):
        nblk = S // (B * d)
        rows = lambda start: pl.ds(start, B, stride=d) if d > 1 else pl.ds(start, B)

        def body(g, _, bi=bi, d=d, nblk=nblk, rows=rows):
            starts, masks = [], []
            for u in range(DIL_GROUP):
                c = g * DIL_GROUP + u
                blk = c % nblk
                start = blk * (B * d) + c // nblk
                starts.append(start)
                masks.append(ki >= qi + jnp.where(jnp.logical_and(first, blk == 0), B, 0))
            sps, scs = [], []
            for start in starts:
                q = q_ref[rows(start), :].astype(BF16)
                kp = kcat[rows(S + start - B * d), :].astype(BF16)
                kc = kcat[rows(S + start), :].astype(BF16)
                sps.append(_dot_nt(q, kp) * scale)
                scs.append(_dot_nt(q, kc) * scale)
            pps, pcs, stats = [], [], []
            for u in range(DIL_GROUP):
                sp = jnp.where(masks[u], sps[u], -jnp.inf)
                sc = jnp.where(in_cur, scs[u], -jnp.inf)
                m = jnp.maximum(jnp.max(sp, axis=-1, keepdims=True), jnp.max(sc, axis=-1, keepdims=True))
                pp = jnp.exp(sp - m)
                pc = jnp.exp(sc - m)
                s = jnp.sum(pp, axis=-1, keepdims=True) + jnp.sum(pc, axis=-1, keepdims=True)
                pps.append(pp.astype(BF16))
                pcs.append(pc.astype(BF16))
                stats.append((m, s))
            for u, start in enumerate(starts):
                vp = vcat[rows(S + start - B * d), :].astype(BF16)
                vc = vcat[rows(S + start), :].astype(BF16)
                ob[bi, rows(start), :] = _dot(pps[u], vp) + _dot(pcs[u], vc)
                mb[bi, rows(start), :] = jnp.broadcast_to(stats[u][0], (B, LANES))
                sb[bi, rows(start), :] = jnp.broadcast_to(stats[u][1], (B, LANES))
            return 0

        lax.fori_loop(0, (d * nblk) // DIL_GROUP, body, 0)

    n_br = len(DIL_PATTERNS)
    chunk = 256

    def combine(c, _):
        r = pl.ds(pl.multiple_of(c * chunk, chunk), chunk)
        ms = [mb[b, r, :] for b in range(n_br)]
        m_max = functools.reduce(jnp.maximum, ms)
        ws = [jnp.exp(m - m_max) for m in ms]
        num = sum(ws[b] * ob[b, r, :] for b in range(n_br))
        den = sum(ws[b] * sb[b, r, :] for b in range(n_br))
        o_ref[r, :] = num / den
        return 0

    lax.fori_loop(0, S // chunk, combine, 0)


def _dil_attention(qkv, n_heads):
    L = qkv.shape[0]
    S = DIL_SUPER
    assert L % S == 0 and all(w // d == DIL_BLOCK and S % (DIL_BLOCK * d) == 0
                              and (S // DIL_BLOCK) % DIL_GROUP == 0 for w, d in DIL_PATTERNS)
    cur = lambda which: pl.BlockSpec((S, HEAD_DIM), lambda h, n: (n, which * n_heads + h))
    prev = lambda which: pl.BlockSpec((S, HEAD_DIM), lambda h, n: (jnp.maximum(n - 1, 0), which * n_heads + h))
    n_br = len(DIL_PATTERNS)
    return pl.pallas_call(
        functools.partial(_dil_kernel, scale=HEAD_DIM ** -0.5),
        grid=(n_heads, L // S),
        in_specs=[cur(0), cur(1), cur(2), prev(1), prev(2)],
        out_specs=pl.BlockSpec((S, HEAD_DIM), lambda h, n: (n, h)),
        out_shape=jax.ShapeDtypeStruct((L, n_heads * HEAD_DIM), F32),
        scratch_shapes=[pltpu.VMEM((2 * S, HEAD_DIM), F32), pltpu.VMEM((2 * S, HEAD_DIM), F32),
                        pltpu.VMEM((n_br, S, HEAD_DIM), F32), pltpu.VMEM((n_br, S, LANES), F32),
                        pltpu.VMEM((n_br, S, LANES), F32)],
        compiler_params=_params("parallel", "arbitrary"),
        name="dilated_attention",
    )(qkv, qkv, qkv, qkv, qkv)


def _s5_prep_kernel(lre_ref, lim_ref, ldt_ref, bre_ref, bim_ref, pre_ref, pim_ref, bbre_ref, bbim_ref):
    a_re = jnp.minimum(lre_ref[...], -1e-4)
    a_im = lim_ref[...]
    dt = jnp.exp(ldt_ref[...])
    mag = jnp.exp(dt * a_re)
    ang = dt * a_im
    abar_re = mag * jnp.cos(ang)
    abar_im = mag * jnp.sin(ang)
    den = a_re * a_re + a_im * a_im
    nr = abar_re - 1.0
    f_re = (nr * a_re + abar_im * a_im) / den
    f_im = (abar_im * a_re - nr * a_im) / den
    for c in range(bre_ref.shape[0]):
        br = bre_ref[c]
        bi = bim_ref[c]
        bbre_ref[c] = f_re * br - f_im * bi
        bbim_ref[c] = f_re * bi + f_im * br
    pr, pi = abar_re, abar_im
    pre_ref[0] = pr
    pim_ref[0] = pi
    for k in range(1, SUBLANES):
        pr, pi = pr * abar_re - pi * abar_im, pr * abar_im + pi * abar_re
        pre_ref[k] = pr
        pim_ref[k] = pi


def _s5_scan_kernel(u_ref, wb_ref, wc_ref, pre_ref, pim_ref, d_ref, y_ref, x_ref, carry_ref):
    S = STATES_PER_TILE
    tc = u_ref.shape[0]

    @pl.when(pl.program_id(1) == 0)
    def _():
        carry_ref[...] = jnp.zeros_like(carry_ref)

    u = u_ref[...]
    x_ref[...] = _dot(u.astype(BF16), wb_ref[...])

    p_re = pre_ref[...]
    p_im = pim_ref[...]
    sub = lax.broadcasted_iota(jnp.int32, (SUBLANES, S), 0)
    steps = []
    for shift in (1, 2, 4):
        a_r = jnp.where(sub >= shift, p_re[shift - 1:shift, :], 0.0)
        a_i = jnp.where(sub >= shift, p_im[shift - 1:shift, :], 0.0)
        steps.append((shift, a_r, a_i))

    def body(b, carry):
        c_re, c_im = carry
        r0 = pl.multiple_of(b * SUBLANES, SUBLANES)
        xr = x_ref[pl.ds(r0, SUBLANES), :S]
        xi = x_ref[pl.ds(r0, SUBLANES), S:]
        for shift, a_r, a_i in steps:
            sr = pltpu.roll(xr, shift, axis=0)
            si = pltpu.roll(xi, shift, axis=0)
            xr, xi = xr + a_r * sr - a_i * si, xi + a_r * si + a_i * sr
        xr, xi = xr + p_re * c_re - p_im * c_im, xi + p_re * c_im + p_im * c_re
        x_ref[pl.ds(r0, SUBLANES), :S] = xr
        x_ref[pl.ds(r0, SUBLANES), S:] = xi
        return xr[SUBLANES - 1:, :], xi[SUBLANES - 1:, :]

    c_re, c_im = lax.fori_loop(0, tc // SUBLANES, body, (carry_ref[0:1, :S], carry_ref[0:1, S:]))
    carry_ref[0:1, :S] = c_re
    carry_ref[0:1, S:] = c_im

    y_ref[...] = _dot(x_ref[...].astype(BF16), wc_ref[...]) + d_ref[...] * u


def _s5_glu_kernel(y_ref, w_ref, b_ref, o_ref):
    g = jax.nn.gelu(y_ref[...])
    o_ref[...] = g * jax.nn.sigmoid(_dot(g.astype(BF16), w_ref[...]) + b_ref[...])


def _block_diag_tiles(m):
    G, a, b = m.shape
    T = G // GROUPS_PER_TILE
    m = m.reshape(T, GROUPS_PER_TILE, a, b)
    eye = jnp.eye(GROUPS_PER_TILE, dtype=m.dtype)
    return jnp.einsum("tgab,gh->tgahb", m, eye).reshape(T, GROUPS_PER_TILE * a, GROUPS_PER_TILE * b)


def _s5_layer(u_src, u_col0, lam_re, lam_im, log_dt, b_re, b_im, c_re, c_im, d_skip, w_glu, b_glu, layer):
    L = u_src.shape[0]
    G, N = lam_re.shape
    C = G * SSM_GROUP
    T = G // GROUPS_PER_TILE
    S = STATES_PER_TILE
    assert N == SSM_STATE and C % LANES == 0 and u_col0 % LANES == 0
    ub = u_col0 // LANES

    full = lambda shape: pl.BlockSpec(shape, lambda: (0,) * len(shape))
    p_re, p_im, bb_re, bb_im = pl.pallas_call(
        _s5_prep_kernel,
        in_specs=[full((G, N)), full((G, N)), full((G, N)), full((SSM_GROUP, G, N)), full((SSM_GROUP, G, N))],
        out_specs=[full((SUBLANES, G, N)), full((SUBLANES, G, N)),
                   full((SSM_GROUP, G, N)), full((SSM_GROUP, G, N))],
        out_shape=[jax.ShapeDtypeStruct((SUBLANES, G, N), F32)] * 2
                  + [jax.ShapeDtypeStruct((SSM_GROUP, G, N), F32)] * 2,
        name="s5_prep",
    )(lam_re, lam_im, jnp.broadcast_to(log_dt[:, None], (G, N)),
      jnp.transpose(b_re, (2, 0, 1)), jnp.transpose(b_im, (2, 0, 1)))

    bb = lambda t: _block_diag_tiles(jnp.transpose(t, (1, 0, 2)))
    wb = jnp.concatenate([bb(bb_re), bb(bb_im)], axis=-1).astype(BF16)
    cc = lambda t: _block_diag_tiles(jnp.transpose(t, (0, 2, 1)))
    wc = jnp.concatenate([cc(c_re), -cc(c_im)], axis=1).astype(BF16)
    pw = lambda t: jnp.transpose(t.reshape(SUBLANES, T, S), (1, 0, 2))

    tc = _pick(L, (1024, 512, 256, 128))
    y = pl.pallas_call(
        _s5_scan_kernel,
        grid=(T, L // tc),
        in_specs=[
            pl.BlockSpec((tc, LANES), lambda c, t: (t, ub + c)),
            pl.BlockSpec((None, LANES, 2 * S), lambda c, t: (c, 0, 0)),
            pl.BlockSpec((None, 2 * S, LANES), lambda c, t: (c, 0, 0)),
            pl.BlockSpec((None, SUBLANES, S), lambda c, t: (c, 0, 0)),
            pl.BlockSpec((None, SUBLANES, S), lambda c, t: (c, 0, 0)),
            pl.BlockSpec((1, LANES), lambda c, t: (0, c)),
        ],
        out_specs=pl.BlockSpec((tc, LANES), lambda c, t: (t, c)),
        out_shape=jax.ShapeDtypeStruct((L, C), F32),
        scratch_shapes=[pltpu.VMEM((tc, 2 * S), F32), pltpu.VMEM((SUBLANES, 2 * S), F32)],
        compiler_params=_params("parallel", "arbitrary"),
        name="s5_scan",
    )(u_src, wb, wc, pw(p_re), pw(p_im), d_skip)

    tm = _pick(L, (512, 256, 128))
    return pl.pallas_call(
        _s5_glu_kernel,
        grid=(L // tm,),
        in_specs=[pl.BlockSpec((tm, C), lambda i: (i, 0)),
                  pl.BlockSpec((None, C, C), lambda i: (layer, 0, 0)),
                  pl.BlockSpec((1, C), lambda i: (0, 0))],
        out_specs=pl.BlockSpec((tm, C), lambda i: (i, 0)),
        out_shape=jax.ShapeDtypeStruct((L, C), F32),
        compiler_params=_params("parallel"),
        name="s5_glu",
    )(y, w_glu, b_glu)


def kernel(x, norm_mix_pre, norm_mix_post, norm_ffn_pre, norm_ffn_post, w_in, norm_out_sb, norm_out_dil, norm_out_ssm, ssm_lambda_re, ssm_lambda_im, ssm_log_dt, ssm_b_re, ssm_b_im, ssm_c_re, ssm_c_im, ssm_d, ssm_w_glu, ssm_b_glu, w_out, ffn_w_gate, ffn_w_up, ffn_w_down):
    B, L, D = x.shape
    depth = w_in.shape[0]
    ssm_c = ssm_d.shape[-1]
    sb_w = norm_out_sb.shape[-1]
    dl_w = norm_out_dil.shape[-1]
    n_sb = sb_w // HEAD_DIM
    n_dl = dl_w // HEAD_DIM
    assert w_in.shape[-1] == 3 * sb_w + 3 * dl_w + ssm_c
    big = (1024, 512, 256, 128)

    w_in_b = w_in.astype(BF16)
    w_out_b = w_out.astype(BF16)
    w_gate_b = ffn_w_gate.astype(BF16)
    w_up_b = ffn_w_up.astype(BF16)
    w_down_b = ffn_w_down.astype(BF16)
    w_glu_b = ssm_w_glu.astype(BF16)
    row = lambda a, l: a[l][None, :]

    outs = []
    for b in range(B):
        xb = x[b]
        h = _norm(xb, row(norm_mix_pre, 0))
        for l in range(depth):
            qkv_sb = _mm(h, w_in_b, l, 0, 3 * sb_w, BF16, big, (512, 384, 256, 128))
            rest = _mm(h, w_in_b, l, 3 * sb_w, 3 * dl_w + ssm_c, F32, big, (512, 256, 128))
            o_sb = _sb_attention(qkv_sb, n_sb)
            o_dl = _dil_attention(rest, n_dl)
            o_ssm = _s5_layer(rest, 3 * dl_w, ssm_lambda_re[l], ssm_lambda_im[l], ssm_log_dt[l],
                              ssm_b_re[l], ssm_b_im[l], ssm_c_re[l], ssm_c_im[l], row(ssm_d, l),
                              w_glu_b, row(ssm_b_glu, l), l)
            y = _out_proj(o_sb, o_dl, o_ssm, row(norm_out_sb, l), row(norm_out_dil, l), row(norm_out_ssm, l),
                          w_out_b, l)
            xb, h = _resid_norm(xb, y, row(norm_mix_post, l), row(norm_ffn_pre, l))
            f = _ffn_in(h, w_gate_b, w_up_b, l)
            y = _mm(f, w_down_b, l, 0, D, F32, (512, 256, 128), (512, 256, 128))
            if l + 1 < depth:
                xb, h = _resid_norm(xb, y, row(norm_ffn_post, l), row(norm_mix_pre, l + 1))
            else:
                xb = _resid_norm(xb, y, row(norm_ffn_post, l))
        outs.append(xb)
    return jnp.stack(outs)
```

```python
import functools

import jax
import jax.numpy as jnp
from jax import lax
from jax.experimental import pallas as pl
from jax.experimental.pallas import tpu as pltpu

HEAD_DIM = 128
SSM_GROUP = 16
SSM_STATE = 64
DIL_BLOCK = 128
DIL_PATTERNS = ((128, 1), (512, 4), (2048, 16))
RMS_EPS = 1e-6

LANES = 128
SUBLANES = 8
GROUPS_PER_TILE = LANES // SSM_GROUP
STATES_PER_TILE = GROUPS_PER_TILE * SSM_STATE
V7X_VMEM_LIMIT_BYTES = 56 * 1024 * 1024

SB_UNDERFLOW = 110.0
SB_HEADS_PER_STEP = 4
DIL_SUPER = DIL_BLOCK * max(d for _, d in DIL_PATTERNS)
DIL_GROUP = 16

F32 = jnp.float32
BF16 = jnp.bfloat16


def _pick(n, candidates):
    for c in candidates:
        if n % c == 0:
            return c
    raise ValueError(f"no tile in {candidates} divides {n}")


def _params(*sem):
    return pltpu.CompilerParams(dimension_semantics=sem, vmem_limit_bytes=V7X_VMEM_LIMIT_BYTES)


def _rms(x, g):
    return x * lax.rsqrt(jnp.mean(x * x, axis=-1, keepdims=True) + RMS_EPS) * g


def _dot(a, b):
    return jnp.dot(a, b, preferred_element_type=F32)


def _dot_nt(a, b):
    return lax.dot_general(a, b, (((1,), (1,)), ((), ())), preferred_element_type=F32)


def _norm_kernel(x_ref, g_ref, h_ref):
    h_ref[...] = _rms(x_ref[...], g_ref[...]).astype(h_ref.dtype)


def _norm(x, g):
    L, D = x.shape
    tm = _pick(L, (256, 128))
    row = pl.BlockSpec((tm, D), lambda i: (i, 0))
    return pl.pallas_call(
        _norm_kernel,
        grid=(L // tm,),
        in_specs=[row, pl.BlockSpec((1, D), lambda i: (0, 0))],
        out_specs=row,
        out_shape=jax.ShapeDtypeStruct((L, D), BF16),
        compiler_params=_params("parallel"),
        name="norm",
    )(x, g)


def _resid_norm_kernel(x_ref, y_ref, g_ref, o_ref):
    o_ref[...] = x_ref[...] + _rms(y_ref[...], g_ref[...])


def _resid_norm_next_kernel(x_ref, y_ref, g_ref, gn_ref, o_ref, h_ref):
    x = x_ref[...] + _rms(y_ref[...], g_ref[...])
    o_ref[...] = x
    h_ref[...] = _rms(x, gn_ref[...]).astype(h_ref.dtype)


def _resid_norm(x, y, g, g_next=None):
    L, D = x.shape
    tm = _pick(L, (256, 128))
    row = pl.BlockSpec((tm, D), lambda i: (i, 0))
    gain = pl.BlockSpec((1, D), lambda i: (0, 0))
    if g_next is None:
        return pl.pallas_call(
            _resid_norm_kernel,
            grid=(L // tm,),
            in_specs=[row, row, gain],
            out_specs=row,
            out_shape=jax.ShapeDtypeStruct((L, D), F32),
            compiler_params=_params("parallel"),
            name="resid_norm",
        )(x, y, g)
    return pl.pallas_call(
        _resid_norm_next_kernel,
        grid=(L // tm,),
        in_specs=[row, row, gain, gain],
        out_specs=[row, row],
        out_shape=[jax.ShapeDtypeStruct((L, D), F32), jax.ShapeDtypeStruct((L, D), BF16)],
        compiler_params=_params("parallel"),
        name="resid_norm_next",
    )(x, y, g, g_next)


def _mm_kernel(a_ref, w_ref, o_ref):
    o_ref[...] = _dot(a_ref[...], w_ref[...].astype(BF16)).astype(o_ref.dtype)


def _mm(a, w, layer, col0, ncols, out_dtype, tms, tns):
    L, K = a.shape
    tm = _pick(L, tms)
    tn = _pick(ncols, tns)
    assert col0 % tn == 0
    jb = col0 // tn
    return pl.pallas_call(
        _mm_kernel,
        grid=(L // tm, ncols // tn),
        in_specs=[
            pl.BlockSpec((tm, K), lambda i, j: (i, 0)),
            pl.BlockSpec((None, K, tn), lambda i, j: (layer, 0, j + jb)),
        ],
        out_specs=pl.BlockSpec((tm, tn), lambda i, j: (i, j)),
        out_shape=jax.ShapeDtypeStruct((L, ncols), out_dtype),
        compiler_params=_params("parallel", "arbitrary"),
        name="mm",
    )(a, w)


def _ffn_in_kernel(h_ref, wg_ref, wu_ref, o_ref):
    h = h_ref[...]
    gate = _dot(h, wg_ref[...].astype(BF16))
    up = _dot(h, wu_ref[...].astype(BF16))
    o_ref[...] = (jax.nn.silu(gate) * up).astype(o_ref.dtype)


def _ffn_in(h, wg, wu, layer):
    L, D = h.shape
    F = wg.shape[-1]
    tm = _pick(L, (1024, 512, 256, 128))
    tn = _pick(F, (256, 128))
    wspec = pl.BlockSpec((None, D, tn), lambda i, j: (layer, 0, j))
    return pl.pallas_call(
        _ffn_in_kernel,
        grid=(L // tm, F // tn),
        in_specs=[pl.BlockSpec((tm, D), lambda i, j: (i, 0)), wspec, wspec],
        out_specs=pl.BlockSpec((tm, tn), lambda i, j: (i, j)),
        out_shape=jax.ShapeDtypeStruct((L, F), BF16),
        compiler_params=_params("parallel", "arbitrary"),
        name="ffn_in",
    )(h, wg, wu)


def _out_proj_kernel(a_ref, b_ref, c_ref, ga_ref, gb_ref, gc_ref, w_ref, o_ref, h_ref):
    @pl.when(pl.program_id(1) == 0)
    def _():
        wa = a_ref.shape[1]
        wb = b_ref.shape[1]
        h_ref[:, :wa] = _rms(a_ref[...], ga_ref[...]).astype(h_ref.dtype)
        h_ref[:, wa:wa + wb] = _rms(b_ref[...], gb_ref[...]).astype(h_ref.dtype)
        h_ref[:, wa + wb:] = _rms(c_ref[...], gc_ref[...]).astype(h_ref.dtype)

    o_ref[...] = _dot(h_ref[...], w_ref[...])


def _out_proj(a, b, c, ga, gb, gc, w, layer):
    L = a.shape[0]
    K = a.shape[1] + b.shape[1] + c.shape[1]
    N = w.shape[-1]
    tm = _pick(L, (512, 256, 128))
    tn = _pick(N, (1024, 512, 256, 128))
    rows = lambda arr: pl.BlockSpec((tm, arr.shape[1]), lambda i, j: (i, 0))
    gain = lambda arr: pl.BlockSpec((1, arr.shape[1]), lambda i, j: (0, 0))
    return pl.pallas_call(
        _out_proj_kernel,
        grid=(L // tm, N // tn),
        in_specs=[rows(a), rows(b), rows(c), gain(ga), gain(gb), gain(gc),
                  pl.BlockSpec((None, K, tn), lambda i, j: (layer, 0, j))],
        out_specs=pl.BlockSpec((tm, tn), lambda i, j: (i, j)),
        out_shape=jax.ShapeDtypeStruct((L, N), F32),
        scratch_shapes=[pltpu.VMEM((tm, K), BF16)],
        compiler_params=_params("parallel", "arbitrary"),
        name="out_proj",
    )(a, b, c, ga, gb, gc, w)


def _sb_kernel(q_ref, k_ref, v_ref, o_ref, *, tq, hp, scale):
    i = pl.program_id(1)
    heads = range(hp)
    cols = lambda h: slice(h * HEAD_DIM, (h + 1) * HEAD_DIM)
    qs = [q_ref[:, cols(h)] for h in heads]
    row = lax.broadcasted_iota(jnp.int32, (tq, tq), 0)
    col = lax.broadcasted_iota(jnp.int32, (tq, tq), 1)
    from_here = jnp.where(row >= col, 1.0, 0.0).astype(BF16)
    from_here = jnp.concatenate([from_here, from_here], axis=0)
    causal = col < row

    def blocks(off, carries, diagonal):
        zs = [_dot_nt(qs[h], k_ref[pl.ds(off, tq), cols(h)]) * scale for h in heads]
        sps, splits = [], []
        for z in zs:
            sp = jnp.maximum(z, 0.0) + jnp.log(1.0 + jnp.exp(-jnp.abs(z)))
            if diagonal:
                sp = jnp.where(causal, sp, 0.0)
            hi = sp.astype(BF16)
            lo = (sp - hi.astype(F32)).astype(BF16)
            sps.append(sp)
            splits.append(jnp.concatenate([hi, lo], axis=1))
        suffixes = [_dot(s, from_here) for s in splits]
        pvs = []
        for h in heads:
            a = jnp.exp(zs[h] - suffixes[h] - carries[h])
            if diagonal:
                a = jnp.where(causal, a, 0.0)
            pvs.append(_dot(a.astype(BF16), v_ref[pl.ds(off, tq), cols(h)]))
        carries = [carries[h] + jnp.sum(sps[h], axis=-1, keepdims=True) for h in heads]
        return pvs, carries

    def live(carries):
        return jnp.min(jnp.broadcast_to(functools.reduce(jnp.minimum, carries), (tq, LANES)))

    accs, carries = blocks(pl.multiple_of(i * tq, tq), [jnp.zeros((tq, 1), F32)] * hp, True)

    def cond(st):
        return jnp.logical_and(st[0] < i, st[1] < SB_UNDERFLOW)

    def body(st):
        s, _, accs, carries = st
        pvs, carries = blocks(pl.multiple_of((i - 1 - s) * tq, tq), carries, False)
        return s + 1, live(carries), [a + p for a, p in zip(accs, pvs)], carries

    _, _, accs, _ = lax.while_loop(cond, body, (jnp.int32(0), live(carries), accs, carries))
    for h in heads:
        o_ref[:, cols(h)] = accs[h]


def _sb_attention(qkv, n_heads):
    L = qkv.shape[0]
    tq = _pick(L, (256, 128))
    hp = _pick(n_heads, (SB_HEADS_PER_STEP, 1))
    w = hp * HEAD_DIM
    nb = n_heads // hp
    kern = functools.partial(_sb_kernel, tq=tq, hp=hp, scale=HEAD_DIM ** -0.5)
    return pl.pallas_call(
        kern,
        grid=(nb, L // tq),
        in_specs=[
            pl.BlockSpec((tq, w), lambda h, i: (i, h)),
            pl.BlockSpec((L, w), lambda h, i: (0, nb + h)),
            pl.BlockSpec((L, w), lambda h, i: (0, 2 * nb + h)),
        ],
        out_specs=pl.BlockSpec((tq, w), lambda h, i: (i, h)),
        out_shape=jax.ShapeDtypeStruct((L, n_heads * HEAD_DIM), F32),
        compiler_params=_params("parallel", "arbitrary"),
        name="sb_attention",
    )(qkv, qkv, qkv)


def _dil_kernel(q_ref, kc_ref, vc_ref, kp_ref, vp_ref, o_ref, kcat, vcat, ob, mb, sb, *, scale):
    S = DIL_SUPER
    B = DIL_BLOCK
    first = pl.program_id(1) == 0
    kcat[:S] = kp_ref[...]
    kcat[S:] = kc_ref[...]
    vcat[:S] = vp_ref[...]
    vcat[S:] = vc_ref[...]
    qi = lax.broadcasted_iota(jnp.int32, (B, B), 0)
    ki = lax.broadcasted_iota(jnp.int32, (B, B), 1)
    in_cur = ki <= qi

    for bi, (window, d) in enumerate(DIL_PATTERNS):
        nblk = S // (B * d)
        rows = lambda start: pl.ds(start, B, stride=d) if d > 1 else pl.ds(start, B)

        def body(g, _, bi=bi, d=d, nblk=nblk, rows=rows):
            starts, masks = [], []
            for u in range(DIL_GROUP):
                c = g * DIL_GROUP + u
                blk = c % nblk
                start = blk * (B * d) + c // nblk
                starts.append(start)
                masks.append(ki >= qi + jnp.where(jnp.logical_and(first, blk == 0), B, 0))
            sps, scs = [], []
            for start in starts:
                q = q_ref[rows(start), :].astype(BF16)
                kp = kcat[rows(S + start - B * d), :].astype(BF16)
                kc = kcat[rows(S + start), :].astype(BF16)
                sps.append(_dot_nt(q, kp) * scale)
                scs.append(_dot_nt(q, kc) * scale)
            pps, pcs, stats = [], [], []
            for u in range(DIL_GROUP):
                sp = jnp.where(masks[u], sps[u], -jnp.inf)
                sc = jnp.where(in_cur, scs[u], -jnp.inf)
                m = jnp.max(jnp.maximum(sp, sc), axis=-1, keepdims=True)
                pp = jnp.exp(sp - m)
                pc = jnp.exp(sc - m)
                s = jnp.sum(pp + pc, axis=-1, keepdims=True)
                pps.append(pp.astype(BF16))
                pcs.append(pc.astype(BF16))
                stats.append((m, s))
            for u, start in enumerate(starts):
                vp = vcat[rows(S + start - B * d), :].astype(BF16)
                vc = vcat[rows(S + start), :].astype(BF16)
                ob[bi, rows(start), :] = _dot(pps[u], vp) + _dot(pcs[u], vc)
                mb[bi, rows(start), :] = jnp.broadcast_to(stats[u][0], (B, LANES))
                sb[bi, rows(start), :] = jnp.broadcast_to(stats[u][1], (B, LANES))
            return 0

        lax.fori_loop(0, (d * nblk) // DIL_GROUP, body, 0)

    n_br = len(DIL_PATTERNS)
    chunk = 256

    def combine(c, _):
        r = pl.ds(pl.multiple_of(c * chunk, chunk), chunk)
        ms = [mb[b, r, :] for b in range(n_br)]
        m_max = functools.reduce(jnp.maximum, ms)
        ws = [jnp.exp(m - m_max) for m in ms]
        num = sum(ws[b] * ob[b, r, :] for b in range(n_br))
        den = sum(ws[b] * sb[b, r, :] for b in range(n_br))
        o_ref[r, :] = num / den
        return 0

    lax.fori_loop(0, S // chunk, combine, 0)


def _dil_attention(qkv, n_heads):
    L = qkv.shape[0]
    S = DIL_SUPER
    assert L % S == 0 and all(w // d == DIL_BLOCK and S % (DIL_BLOCK * d) == 0
                              and (S // DIL_BLOCK) % DIL_GROUP == 0 for w, d in DIL_PATTERNS)
    cur = lambda which: pl.BlockSpec((S, HEAD_DIM), lambda h, n: (n, which * n_heads + h))
    prev = lambda which: pl.BlockSpec((S, HEAD_DIM), lambda h, n: (jnp.maximum(n - 1, 0), which * n_heads + h))
    n_br = len(DIL_PATTERNS)
    return pl.pallas_call(
        functools.partial(_dil_kernel, scale=HEAD_DIM ** -0.5),
        grid=(n_heads, L // S),
        in_specs=[cur(0), cur(1), cur(2), prev(1), prev(2)],
        out_specs=pl.BlockSpec((S, HEAD_DIM), lambda h, n: (n, h)),
        out_shape=jax.ShapeDtypeStruct((L, n_heads * HEAD_DIM), F32),
        scratch_shapes=[pltpu.VMEM((2 * S, HEAD_DIM), F32), pltpu.VMEM((2 * S, HEAD_DIM), F32),
                        pltpu.VMEM((n_br, S, HEAD_DIM), F32), pltpu.VMEM((n_br, S, LANES), F32),
                        pltpu.VMEM((n_br, S, LANES), F32)],
        compiler_params=_params("parallel", "arbitrary"),
        name="dilated_attention",
    )(qkv, qkv, qkv, qkv, qkv)


def _s5_prep_kernel(lre_ref, lim_ref, ldt_ref, bre_ref, bim_ref, pre_ref, pim_ref, bbre_ref, bbim_ref):
    a_re = jnp.minimum(lre_ref[...], -1e-4)
    a_im = lim_ref[...]
    dt = jnp.exp(ldt_ref[...])
    mag = jnp.exp(dt * a_re)
    ang = dt * a_im
    abar_re = mag * jnp.cos(ang)
    abar_im = mag * jnp.sin(ang)
    den = a_re * a_re + a_im * a_im
    nr = abar_re - 1.0
    f_re = (nr * a_re + abar_im * a_im) / den
    f_im = (abar_im * a_re - nr * a_im) / den
    for c in range(bre_ref.shape[0]):
        br = bre_ref[c]
        bi = bim_ref[c]
        bbre_ref[c] = f_re * br - f_im * bi
        bbim_ref[c] = f_re * bi + f_im * br
    pre_ref[0] = abar_re
    pim_ref[0] = abar_im

    def power(k, p):
        pr, pi = p
        pr, pi = pr * abar_re - pi * abar_im, pr * abar_im + pi * abar_re
        pre_ref[k] = pr
        pim_ref[k] = pi
        return pr, pi

    lax.fori_loop(1, pre_ref.shape[0], power, (abar_re, abar_im))


def _s5_scan_kernel(u_ref, wb_ref, wc_ref, pre_ref, pim_ref, d_ref, y_ref, up_ref, x_ref, yp_ref, carry_ref):
    S = STATES_PER_TILE
    tc = u_ref.shape[0]
    J = tc // SUBLANES

    @pl.when(pl.program_id(1) == 0)
    def _():
        carry_ref[...] = jnp.zeros_like(carry_ref)

    for s in range(SUBLANES):
        up_ref[pl.ds(s, J, stride=SUBLANES), :] = u_ref[pl.ds(s * J, J), :]
    x_ref[...] = _dot(up_ref[...].astype(BF16), wb_ref[...])

    a_re = jnp.broadcast_to(pre_ref[0:1, :], (SUBLANES, S))
    a_im = jnp.broadcast_to(pim_ref[0:1, :], (SUBLANES, S))

    def scan(j, c):
        c_re, c_im = c
        r = pl.ds(pl.multiple_of(j * SUBLANES, SUBLANES), SUBLANES)
        c_re, c_im = (a_re * c_re - a_im * c_im + x_ref[r, :S],
                      a_re * c_im + a_im * c_re + x_ref[r, S:])
        x_ref[r, :S] = c_re
        x_ref[r, S:] = c_im
        return c_re, c_im

    zero = jnp.zeros((SUBLANES, S), F32)
    e_re, e_im = lax.fori_loop(0, J, scan, (zero, zero), unroll=2)

    aj_re = pre_ref[J - 1:J, :]
    aj_im = pim_ref[J - 1:J, :]
    g_re = carry_ref[0:1, :S]
    g_im = carry_ref[0:1, S:]
    sub = lax.broadcasted_iota(jnp.int32, (SUBLANES, S), 0)
    start_re = zero
    start_im = zero
    for s in range(SUBLANES):
        start_re = jnp.where(sub == s, g_re, start_re)
        start_im = jnp.where(sub == s, g_im, start_im)
        g_re, g_im = (e_re[s:s + 1, :] + aj_re * g_re - aj_im * g_im,
                      e_im[s:s + 1, :] + aj_re * g_im + aj_im * g_re)
    carry_ref[0:1, :S] = g_re
    carry_ref[0:1, S:] = g_im

    def fix(j, _):
        r = pl.ds(pl.multiple_of(j * SUBLANES, SUBLANES), SUBLANES)
        p_re = pre_ref[pl.ds(j, 1), :]
        p_im = pim_ref[pl.ds(j, 1), :]
        x_ref[r, :S] = x_ref[r, :S] + p_re * start_re - p_im * start_im
        x_ref[r, S:] = x_ref[r, S:] + p_re * start_im + p_im * start_re
        return 0

    lax.fori_loop(0, J, fix, 0, unroll=2)

    yp_ref[...] = _dot(x_ref[...].astype(BF16), wc_ref[...])
    for s in range(SUBLANES):
        rows = pl.ds(s * J, J)
        y_ref[rows, :] = yp_ref[pl.ds(s, J, stride=SUBLANES), :] + d_ref[...] * u_ref[rows, :]


def _s5_glu_kernel(y_ref, w_ref, b_ref, o_ref):
    g = jax.nn.gelu(y_ref[...])
    o_ref[...] = g * jax.nn.sigmoid(_dot(g.astype(BF16), w_ref[...]) + b_ref[...])


def _block_diag_tiles(m):
    G, a, b = m.shape
    T = G // GROUPS_PER_TILE
    m = m.reshape(T, GROUPS_PER_TILE, a, b)
    eye = jnp.eye(GROUPS_PER_TILE, dtype=m.dtype)
    return jnp.einsum("tgab,gh->tgahb", m, eye).reshape(T, GROUPS_PER_TILE * a, GROUPS_PER_TILE * b)


def _s5_layer(u_src, u_col0, lam_re, lam_im, log_dt, b_re, b_im, c_re, c_im, d_skip, w_glu, b_glu, layer):
    L = u_src.shape[0]
    G, N = lam_re.shape
    C = G * SSM_GROUP
    T = G // GROUPS_PER_TILE
    S = STATES_PER_TILE
    assert N == SSM_STATE and C % LANES == 0 and u_col0 % LANES == 0
    ub = u_col0 // LANES
    tc = _pick(L, (1024, 512, 256, 128))
    n_pow = tc // SUBLANES

    full = lambda shape: pl.BlockSpec(shape, lambda: (0,) * len(shape))
    p_re, p_im, bb_re, bb_im = pl.pallas_call(
        _s5_prep_kernel,
        in_specs=[full((G, N)), full((G, N)), full((G, N)), full((SSM_GROUP, G, N)), full((SSM_GROUP, G, N))],
        out_specs=[full((n_pow, G, N)), full((n_pow, G, N)),
                   full((SSM_GROUP, G, N)), full((SSM_GROUP, G, N))],
        out_shape=[jax.ShapeDtypeStruct((n_pow, G, N), F32)] * 2
                  + [jax.ShapeDtypeStruct((SSM_GROUP, G, N), F32)] * 2,
        name="s5_prep",
    )(lam_re, lam_im, jnp.broadcast_to(log_dt[:, None], (G, N)),
      jnp.transpose(b_re, (2, 0, 1)), jnp.transpose(b_im, (2, 0, 1)))

    bb = lambda t: _block_diag_tiles(jnp.transpose(t, (1, 0, 2)))
    wb = jnp.concatenate([bb(bb_re), bb(bb_im)], axis=-1).astype(BF16)
    cc = lambda t: _block_diag_tiles(jnp.transpose(t, (0, 2, 1)))
    wc = jnp.concatenate([cc(c_re), -cc(c_im)], axis=1).astype(BF16)
    pw = lambda t: jnp.transpose(t.reshape(n_pow, T, S), (1, 0, 2))

    y = pl.pallas_call(
        _s5_scan_kernel,
        grid=(T, L // tc),
        in_specs=[
            pl.BlockSpec((tc, LANES), lambda c, t: (t, ub + c)),
            pl.BlockSpec((None, LANES, 2 * S), lambda c, t: (c, 0, 0)),
            pl.BlockSpec((None, 2 * S, LANES), lambda c, t: (c, 0, 0)),
            pl.BlockSpec((None, n_pow, S), lambda c, t: (c, 0, 0)),
            pl.BlockSpec((None, n_pow, S), lambda c, t: (c, 0, 0)),
            pl.BlockSpec((1, LANES), lambda c, t: (0, c)),
        ],
        out_specs=pl.BlockSpec((tc, LANES), lambda c, t: (t, c)),
        out_shape=jax.ShapeDtypeStruct((L, C), F32),
        scratch_shapes=[pltpu.VMEM((tc, LANES), F32), pltpu.VMEM((tc, 2 * S), F32),
                        pltpu.VMEM((tc, LANES), F32), pltpu.VMEM((SUBLANES, 2 * S), F32)],
        compiler_params=_params("parallel", "arbitrary"),
        name="s5_scan",
    )(u_src, wb, wc, pw(p_re), pw(p_im), d_skip)

    tm = _pick(L, (512, 256, 128))
    return pl.pallas_call(
        _s5_glu_kernel,
        grid=(L // tm,),
        in_specs=[pl.BlockSpec((tm, C), lambda i: (i, 0)),
                  pl.BlockSpec((None, C, C), lambda i: (layer, 0, 0)),
                  pl.BlockSpec((1, C), lambda i: (0, 0))],
        out_specs=pl.BlockSpec((tm, C), lambda i: (i, 0)),
        out_shape=jax.ShapeDtypeStruct((L, C), F32),
        compiler_params=_params("parallel"),
        name="s5_glu",
    )(y, w_glu, b_glu)


def kernel(x, norm_mix_pre, norm_mix_post, norm_ffn_pre, norm_ffn_post, w_in, norm_out_sb, norm_out_dil, norm_out_ssm, ssm_lambda_re, ssm_lambda_im, ssm_log_dt, ssm_b_re, ssm_b_im, ssm_c_re, ssm_c_im, ssm_d, ssm_w_glu, ssm_b_glu, w_out, ffn_w_gate, ffn_w_up, ffn_w_down):
    B, L, D = x.shape
    depth = w_in.shape[0]
    ssm_c = ssm_d.shape[-1]
    sb_w = norm_out_sb.shape[-1]
    dl_w = norm_out_dil.shape[-1]
    n_sb = sb_w // HEAD_DIM
    n_dl = dl_w // HEAD_DIM
    assert w_in.shape[-1] == 3 * sb_w + 3 * dl_w + ssm_c
    big = (1024, 512, 256, 128)

    w_out_b = w_out.astype(BF16)
    w_down_b = ffn_w_down.astype(BF16)
    w_glu_b = ssm_w_glu.astype(BF16)
    row = lambda a, l: a[l][None, :]

    outs = []
    for b in range(B):
        xb = x[b]
        h = _norm(xb, row(norm_mix_pre, 0))
        for l in range(depth):
            qkv_sb = _mm(h, w_in, l, 0, 3 * sb_w, BF16, big, (512, 384, 256, 128))
            rest = _mm(h, w_in, l, 3 * sb_w, 3 * dl_w + ssm_c, F32, big, (512, 256, 128))
            o_sb = _sb_attention(qkv_sb, n_sb)
            o_dl = _dil_attention(rest, n_dl)
            o_ssm = _s5_layer(rest, 3 * dl_w, ssm_lambda_re[l], ssm_lambda_im[l], ssm_log_dt[l],
                              ssm_b_re[l], ssm_b_im[l], ssm_c_re[l], ssm_c_im[l], row(ssm_d, l),
                              w_glu_b, row(ssm_b_glu, l), l)
            y = _out_proj(o_sb, o_dl, o_ssm, row(norm_out_sb, l), row(norm_out_dil, l), row(norm_out_ssm, l),
                          w_out_b, l)
            xb, h = _resid_norm(xb, y, row(norm_mix_post, l), row(norm_ffn_pre, l))
            f = _ffn_in(h, ffn_w_gate, ffn_w_up, l)
            y = _mm(f, w_down_b, l, 0, D, F32, (512, 256, 128), (512, 256, 128))
            if l + 1 < depth:
                xb, h = _resid_norm(xb, y, row(norm_ffn_post, l), row(norm_mix_pre, l + 1))
            else:
                xb = _resid_norm(xb, y, row(norm_ffn_post, l))
        outs.append(xb)
    return jnp.stack(outs)
```

```python
import functools

import jax
import jax.numpy as jnp
from jax import lax
from jax.experimental import pallas as pl
from jax.experimental.pallas import tpu as pltpu

HEAD_DIM = 128
SSM_GROUP = 16
SSM_STATE = 64
DIL_BLOCK = 128
DIL_PATTERNS = ((128, 1), (512, 4), (2048, 16))
RMS_EPS = 1e-6

LANES = 128
SUBLANES = 8
GROUPS_PER_TILE = LANES // SSM_GROUP
STATES_PER_TILE = GROUPS_PER_TILE * SSM_STATE
V7X_VMEM_LIMIT_BYTES = 56 * 1024 * 1024

SB_UNDERFLOW = 110.0
SB_HEADS_PER_STEP = 4
DIL_SUPER = DIL_BLOCK * max(d for _, d in DIL_PATTERNS)
DIL_GROUP = 16

F32 = jnp.float32
BF16 = jnp.bfloat16


def _pick(n, candidates):
    for c in candidates:
        if n % c == 0:
            return c
    raise ValueError(f"no tile in {candidates} divides {n}")


def _params(*sem):
    return pltpu.CompilerParams(dimension_semantics=sem, vmem_limit_bytes=V7X_VMEM_LIMIT_BYTES)


def _rms(x, g):
    return x * lax.rsqrt(jnp.mean(x * x, axis=-1, keepdims=True) + RMS_EPS) * g


def _dot(a, b):
    return jnp.dot(a, b, preferred_element_type=F32)


def _dot_nt(a, b):
    return lax.dot_general(a, b, (((1,), (1,)), ((), ())), preferred_element_type=F32)


def _norm_kernel(x_ref, g_ref, h_ref):
    h_ref[...] = _rms(x_ref[...], g_ref[...]).astype(h_ref.dtype)


def _norm(x, g):
    L, D = x.shape
    tm = _pick(L, (256, 128))
    row = pl.BlockSpec((tm, D), lambda i: (i, 0))
    return pl.pallas_call(
        _norm_kernel,
        grid=(L // tm,),
        in_specs=[row, pl.BlockSpec((1, D), lambda i: (0, 0))],
        out_specs=row,
        out_shape=jax.ShapeDtypeStruct((L, D), BF16),
        compiler_params=_params("parallel"),
        name="norm",
    )(x, g)


def _resid_norm_kernel(x_ref, y_ref, g_ref, o_ref):
    o_ref[...] = x_ref[...] + _rms(y_ref[...], g_ref[...])


def _resid_norm_next_kernel(x_ref, y_ref, g_ref, gn_ref, o_ref, h_ref):
    x = x_ref[...] + _rms(y_ref[...], g_ref[...])
    o_ref[...] = x
    h_ref[...] = _rms(x, gn_ref[...]).astype(h_ref.dtype)


def _resid_norm(x, y, g, g_next=None):
    L, D = x.shape
    tm = _pick(L, (256, 128))
    row = pl.BlockSpec((tm, D), lambda i: (i, 0))
    gain = pl.BlockSpec((1, D), lambda i: (0, 0))
    if g_next is None:
        return pl.pallas_call(
            _resid_norm_kernel,
            grid=(L // tm,),
            in_specs=[row, row, gain],
            out_specs=row,
            out_shape=jax.ShapeDtypeStruct((L, D), F32),
            compiler_params=_params("parallel"),
            name="resid_norm",
        )(x, y, g)
    return pl.pallas_call(
        _resid_norm_next_kernel,
        grid=(L // tm,),
        in_specs=[row, row, gain, gain],
        out_specs=[row, row],
        out_shape=[jax.ShapeDtypeStruct((L, D), F32), jax.ShapeDtypeStruct((L, D), BF16)],
        compiler_params=_params("parallel"),
        name="resid_norm_next",
    )(x, y, g, g_next)


def _mm_kernel(a_ref, w_ref, o_ref):
    o_ref[...] = _dot(a_ref[...], w_ref[...].astype(BF16)).astype(o_ref.dtype)


def _mm(a, w, layer, col0, ncols, out_dtype, tms, tns, single_buffer_a=False):
    L, K = a.shape
    tm = _pick(L, tms)
    tn = _pick(ncols, tns)
    assert col0 % tn == 0
    jb = col0 // tn
    a_mode = dict(pipeline_mode=pl.Buffered(1)) if single_buffer_a else {}
    return pl.pallas_call(
        _mm_kernel,
        grid=(L // tm, ncols // tn),
        in_specs=[
            pl.BlockSpec((tm, K), lambda i, j: (i, 0), **a_mode),
            pl.BlockSpec((None, K, tn), lambda i, j: (layer, 0, j + jb)),
        ],
        out_specs=pl.BlockSpec((tm, tn), lambda i, j: (i, j)),
        out_shape=jax.ShapeDtypeStruct((L, ncols), out_dtype),
        compiler_params=_params("parallel", "arbitrary"),
        name="mm",
    )(a, w)


def _ffn_in_kernel(h_ref, wg_ref, wu_ref, o_ref):
    h = h_ref[...]
    gate = _dot(h, wg_ref[...].astype(BF16))
    up = _dot(h, wu_ref[...].astype(BF16))
    o_ref[...] = (jax.nn.silu(gate) * up).astype(o_ref.dtype)


def _ffn_in(h, wg, wu, layer):
    L, D = h.shape
    F = wg.shape[-1]
    tm = _pick(L, (2048, 1024, 512, 256, 128))
    tn = _pick(F, (256, 128))
    wspec = pl.BlockSpec((None, D, tn), lambda i, j: (layer, 0, j))
    return pl.pallas_call(
        _ffn_in_kernel,
        grid=(L // tm, F // tn),
        in_specs=[pl.BlockSpec((tm, D), lambda i, j: (i, 0)), wspec, wspec],
        out_specs=pl.BlockSpec((tm, tn), lambda i, j: (i, j)),
        out_shape=jax.ShapeDtypeStruct((L, F), BF16),
        compiler_params=_params("parallel", "arbitrary"),
        name="ffn_in",
    )(h, wg, wu)


def _out_proj_kernel(a_ref, b_ref, c_ref, ga_ref, gb_ref, gc_ref, w_ref, o_ref, h_ref):
    @pl.when(pl.program_id(1) == 0)
    def _():
        wa = a_ref.shape[1]
        wb = b_ref.shape[1]
        h_ref[:, :wa] = _rms(a_ref[...], ga_ref[...]).astype(h_ref.dtype)
        h_ref[:, wa:wa + wb] = _rms(b_ref[...], gb_ref[...]).astype(h_ref.dtype)
        h_ref[:, wa + wb:] = _rms(c_ref[...], gc_ref[...]).astype(h_ref.dtype)

    o_ref[...] = _dot(h_ref[...], w_ref[...])


def _out_proj(a, b, c, ga, gb, gc, w, layer):
    L = a.shape[0]
    K = a.shape[1] + b.shape[1] + c.shape[1]
    N = w.shape[-1]
    tm = _pick(L, (512, 256, 128))
    tn = _pick(N, (1024, 512, 256, 128))
    rows = lambda arr: pl.BlockSpec((tm, arr.shape[1]), lambda i, j: (i, 0))
    gain = lambda arr: pl.BlockSpec((1, arr.shape[1]), lambda i, j: (0, 0))
    return pl.pallas_call(
        _out_proj_kernel,
        grid=(L // tm, N // tn),
        in_specs=[rows(a), rows(b), rows(c), gain(ga), gain(gb), gain(gc),
                  pl.BlockSpec((None, K, tn), lambda i, j: (layer, 0, j))],
        out_specs=pl.BlockSpec((tm, tn), lambda i, j: (i, j)),
        out_shape=jax.ShapeDtypeStruct((L, N), F32),
        scratch_shapes=[pltpu.VMEM((tm, K), BF16)],
        compiler_params=_params("parallel", "arbitrary"),
        name="out_proj",
    )(a, b, c, ga, gb, gc, w)


def _sb_kernel(q_ref, k_ref, v_ref, o_ref, *, tq, hp, scale):
    i = pl.program_id(1)
    heads = range(hp)
    cols = lambda h: slice(h * HEAD_DIM, (h + 1) * HEAD_DIM)
    qs = [q_ref[:, cols(h)] for h in heads]
    row = lax.broadcasted_iota(jnp.int32, (tq, tq), 0)
    col = lax.broadcasted_iota(jnp.int32, (tq, tq), 1)
    from_here = jnp.where(row >= col, 1.0, 0.0).astype(BF16)
    from_here = jnp.concatenate([from_here, from_here], axis=0)
    causal = col < row

    def blocks(off, carries, diagonal):
        zs = [_dot_nt(qs[h], k_ref[pl.ds(off, tq), cols(h)]) * scale for h in heads]
        sps, splits = [], []
        for z in zs:
            sp = jnp.maximum(z, 0.0) + jnp.log(1.0 + jnp.exp(-jnp.abs(z)))
            if diagonal:
                sp = jnp.where(causal, sp, 0.0)
            hi = sp.astype(BF16)
            lo = (sp - hi.astype(F32)).astype(BF16)
            sps.append(sp)
            splits.append(jnp.concatenate([hi, lo], axis=1))
        suffixes = [_dot(s, from_here) for s in splits]
        pvs = []
        for h in heads:
            a = jnp.exp(zs[h] - suffixes[h] - carries[h])
            if diagonal:
                a = jnp.where(causal, a, 0.0)
            pvs.append(_dot(a.astype(BF16), v_ref[pl.ds(off, tq), cols(h)]))
        carries = [carries[h] + jnp.sum(sps[h], axis=-1, keepdims=True) for h in heads]
        return pvs, carries

    def live(carries):
        return jnp.min(jnp.broadcast_to(functools.reduce(jnp.minimum, carries), (tq, LANES)))

    accs, carries = blocks(pl.multiple_of(i * tq, tq), [jnp.zeros((tq, 1), F32)] * hp, True)

    def cond(st):
        return jnp.logical_and(st[0] < i, st[1] < SB_UNDERFLOW)

    def body(st):
        s, _, accs, carries = st
        pvs, carries = blocks(pl.multiple_of((i - 1 - s) * tq, tq), carries, False)
        return s + 1, live(carries), [a + p for a, p in zip(accs, pvs)], carries

    _, _, accs, _ = lax.while_loop(cond, body, (jnp.int32(0), live(carries), accs, carries))
    for h in heads:
        o_ref[:, cols(h)] = accs[h]


def _sb_attention(qkv, n_heads):
    L = qkv.shape[0]
    tq = _pick(L, (256, 128))
    hp = _pick(n_heads, (SB_HEADS_PER_STEP, 1))
    w = hp * HEAD_DIM
    nb = n_heads // hp
    kern = functools.partial(_sb_kernel, tq=tq, hp=hp, scale=HEAD_DIM ** -0.5)
    return pl.pallas_call(
        kern,
        grid=(nb, L // tq),
        in_specs=[
            pl.BlockSpec((tq, w), lambda h, i: (i, h)),
            pl.BlockSpec((L, w), lambda h, i: (0, nb + h)),
            pl.BlockSpec((L, w), lambda h, i: (0, 2 * nb + h)),
        ],
        out_specs=pl.BlockSpec((tq, w), lambda h, i: (i, h)),
        out_shape=jax.ShapeDtypeStruct((L, n_heads * HEAD_DIM), F32),
        compiler_params=_params("parallel", "arbitrary"),
        name="sb_attention",
    )(qkv, qkv, qkv)


def _dil_kernel(q_ref, kc_ref, vc_ref, kp_ref, vp_ref, o_ref, kcat, vcat, ob, mb, sb, *, scale):
    S = DIL_SUPER
    B = DIL_BLOCK
    first = pl.program_id(1) == 0
    kcat[:S] = kp_ref[...]
    kcat[S:] = kc_ref[...]
    vcat[:S] = vp_ref[...]
    vcat[S:] = vc_ref[...]
    qi = lax.broadcasted_iota(jnp.int32, (B, 2 * B), 0)
    ki = lax.broadcasted_iota(jnp.int32, (B, 2 * B), 1)
    upto_self = ki <= qi + B

    for bi, (window, d) in enumerate(DIL_PATTERNS):
        nblk = S // (B * d)
        rows = lambda start, n: pl.ds(start, n, stride=d) if d > 1 else pl.ds(start, n)

        def body(g, _, bi=bi, d=d, nblk=nblk, rows=rows):
            starts, masks = [], []
            for u in range(DIL_GROUP):
                c = g * DIL_GROUP + u
                blk = c % nblk
                start = blk * (B * d) + c // nblk
                starts.append(start)
                oldest = jnp.maximum(qi, jnp.where(jnp.logical_and(first, blk == 0), B, 0))
                masks.append(jnp.logical_and(ki >= oldest, upto_self))
            scores = []
            for start in starts:
                q = q_ref[rows(start, B), :].astype(BF16)
                k = kcat[rows(S + start - B * d, 2 * B), :].astype(BF16)
                scores.append(_dot_nt(q, k) * scale)
            ps, stats = [], []
            for u in range(DIL_GROUP):
                sc = jnp.where(masks[u], scores[u], -jnp.inf)
                m = jnp.max(sc, axis=-1, keepdims=True)
                p = jnp.exp(sc - m)
                ps.append(p.astype(BF16))
                stats.append((m, jnp.sum(p, axis=-1, keepdims=True)))
            for u, start in enumerate(starts):
                v = vcat[rows(S + start - B * d, 2 * B), :].astype(BF16)
                ob[bi, rows(start, B), :] = _dot(ps[u], v)
                mb[bi, rows(start, B), :] = jnp.broadcast_to(stats[u][0], (B, LANES))
                sb[bi, rows(start, B), :] = jnp.broadcast_to(stats[u][1], (B, LANES))
            return 0

        lax.fori_loop(0, (d * nblk) // DIL_GROUP, body, 0)

    n_br = len(DIL_PATTERNS)
    chunk = 256

    def combine(c, _):
        r = pl.ds(pl.multiple_of(c * chunk, chunk), chunk)
        ms = [mb[b, r, :] for b in range(n_br)]
        m_max = functools.reduce(jnp.maximum, ms)
        ws = [jnp.exp(m - m_max) for m in ms]
        num = sum(ws[b] * ob[b, r, :] for b in range(n_br))
        den = sum(ws[b] * sb[b, r, :] for b in range(n_br))
        o_ref[r, :] = num / den
        return 0

    lax.fori_loop(0, S // chunk, combine, 0)


def _dil_attention(qkv, n_heads):
    L = qkv.shape[0]
    S = DIL_SUPER
    assert L % S == 0 and all(w // d == DIL_BLOCK and S % (DIL_BLOCK * d) == 0
                              and (S // DIL_BLOCK) % DIL_GROUP == 0 for w, d in DIL_PATTERNS)
    cur = lambda which: pl.BlockSpec((S, HEAD_DIM), lambda h, n: (n, which * n_heads + h))
    prev = lambda which: pl.BlockSpec((S, HEAD_DIM), lambda h, n: (jnp.maximum(n - 1, 0), which * n_heads + h))
    n_br = len(DIL_PATTERNS)
    return pl.pallas_call(
        functools.partial(_dil_kernel, scale=HEAD_DIM ** -0.5),
        grid=(n_heads, L // S),
        in_specs=[cur(0), cur(1), cur(2), prev(1), prev(2)],
        out_specs=pl.BlockSpec((S, HEAD_DIM), lambda h, n: (n, h)),
        out_shape=jax.ShapeDtypeStruct((L, n_heads * HEAD_DIM), F32),
        scratch_shapes=[pltpu.VMEM((2 * S, HEAD_DIM), F32), pltpu.VMEM((2 * S, HEAD_DIM), F32),
                        pltpu.VMEM((n_br, S, HEAD_DIM), F32), pltpu.VMEM((n_br, S, LANES), F32),
                        pltpu.VMEM((n_br, S, LANES), F32)],
        compiler_params=_params("parallel", "arbitrary"),
        name="dilated_attention",
    )(qkv, qkv, qkv, qkv, qkv)


def _s5_prep_kernel(lre_ref, lim_ref, ldt_ref, bre_ref, bim_ref, pre_ref, pim_ref, bbre_ref, bbim_ref):
    a_re = jnp.minimum(lre_ref[...], -1e-4)
    a_im = lim_ref[...]
    dt = jnp.exp(ldt_ref[...])
    mag = jnp.exp(dt * a_re)
    ang = dt * a_im
    abar_re = mag * jnp.cos(ang)
    abar_im = mag * jnp.sin(ang)
    den = a_re * a_re + a_im * a_im
    nr = abar_re - 1.0
    f_re = (nr * a_re + abar_im * a_im) / den
    f_im = (abar_im * a_re - nr * a_im) / den
    for c in range(bre_ref.shape[0]):
        br = bre_ref[c]
        bi = bim_ref[c]
        bbre_ref[c] = f_re * br - f_im * bi
        bbim_ref[c] = f_re * bi + f_im * br
    pre_ref[0] = abar_re
    pim_ref[0] = abar_im

    def power(k, p):
        pr, pi = p
        pr, pi = pr * abar_re - pi * abar_im, pr * abar_im + pi * abar_re
        pre_ref[k] = pr
        pim_ref[k] = pi
        return pr, pi

    lax.fori_loop(1, pre_ref.shape[0], power, (abar_re, abar_im))


def _s5_scan_kernel(u_ref, wb_ref, wc_ref, pre_ref, pim_ref, d_ref, y_ref, up_ref, x_ref, yp_ref, carry_ref):
    S = STATES_PER_TILE
    tc = u_ref.shape[0]
    J = tc // SUBLANES

    @pl.when(pl.program_id(1) == 0)
    def _():
        carry_ref[...] = jnp.zeros_like(carry_ref)

    for s in range(SUBLANES):
        up_ref[pl.ds(s, J, stride=SUBLANES), :] = u_ref[pl.ds(s * J, J), :]
    x_ref[...] = _dot(up_ref[...].astype(BF16), wb_ref[...])

    a_re = jnp.broadcast_to(pre_ref[0:1, :], (SUBLANES, S))
    a_im = jnp.broadcast_to(pim_ref[0:1, :], (SUBLANES, S))

    def scan(j, c):
        c_re, c_im = c
        r = pl.ds(pl.multiple_of(j * SUBLANES, SUBLANES), SUBLANES)
        c_re, c_im = (a_re * c_re - a_im * c_im + x_ref[r, :S],
                      a_re * c_im + a_im * c_re + x_ref[r, S:])
        x_ref[r, :S] = c_re
        x_ref[r, S:] = c_im
        return c_re, c_im

    zero = jnp.zeros((SUBLANES, S), F32)
    e_re, e_im = lax.fori_loop(0, J, scan, (zero, zero), unroll=2)

    aj_re = pre_ref[J - 1:J, :]
    aj_im = pim_ref[J - 1:J, :]
    g_re = carry_ref[0:1, :S]
    g_im = carry_ref[0:1, S:]
    sub = lax.broadcasted_iota(jnp.int32, (SUBLANES, S), 0)
    start_re = zero
    start_im = zero
    for s in range(SUBLANES):
        start_re = jnp.where(sub == s, g_re, start_re)
        start_im = jnp.where(sub == s, g_im, start_im)
        g_re, g_im = (e_re[s:s + 1, :] + aj_re * g_re - aj_im * g_im,
                      e_im[s:s + 1, :] + aj_re * g_im + aj_im * g_re)
    carry_ref[0:1, :S] = g_re
    carry_ref[0:1, S:] = g_im

    def fix(j, _):
        r = pl.ds(pl.multiple_of(j * SUBLANES, SUBLANES), SUBLANES)
        p_re = pre_ref[pl.ds(j, 1), :]
        p_im = pim_ref[pl.ds(j, 1), :]
        x_ref[r, :S] = x_ref[r, :S] + p_re * start_re - p_im * start_im
        x_ref[r, S:] = x_ref[r, S:] + p_re * start_im + p_im * start_re
        return 0

    lax.fori_loop(0, J, fix, 0, unroll=2)

    yp_ref[...] = _dot(x_ref[...].astype(BF16), wc_ref[...])
    for s in range(SUBLANES):
        rows = pl.ds(s * J, J)
        y_ref[rows, :] = yp_ref[pl.ds(s, J, stride=SUBLANES), :] + d_ref[...] * u_ref[rows, :]


def _s5_glu_kernel(y_ref, w_ref, b_ref, o_ref):
    g = jax.nn.gelu(y_ref[...])
    o_ref[...] = g * jax.nn.sigmoid(_dot(g.astype(BF16), w_ref[...]) + b_ref[...])


def _block_diag_tiles(m):
    G, a, b = m.shape
    T = G // GROUPS_PER_TILE
    m = m.reshape(T, GROUPS_PER_TILE, a, b)
    eye = jnp.eye(GROUPS_PER_TILE, dtype=m.dtype)
    return jnp.einsum("tgab,gh->tgahb", m, eye).reshape(T, GROUPS_PER_TILE * a, GROUPS_PER_TILE * b)


def _s5_layer(u_src, u_col0, lam_re, lam_im, log_dt, b_re, b_im, c_re, c_im, d_skip, w_glu, b_glu, layer):
    L = u_src.shape[0]
    G, N = lam_re.shape
    C = G * SSM_GROUP
    T = G // GROUPS_PER_TILE
    S = STATES_PER_TILE
    assert N == SSM_STATE and C % LANES == 0 and u_col0 % LANES == 0
    ub = u_col0 // LANES
    tc = _pick(L, (1024, 512, 256, 128))
    n_pow = tc // SUBLANES

    full = lambda shape: pl.BlockSpec(shape, lambda: (0,) * len(shape))
    p_re, p_im, bb_re, bb_im = pl.pallas_call(
        _s5_prep_kernel,
        in_specs=[full((G, N)), full((G, N)), full((G, N)), full((SSM_GROUP, G, N)), full((SSM_GROUP, G, N))],
        out_specs=[full((n_pow, G, N)), full((n_pow, G, N)),
                   full((SSM_GROUP, G, N)), full((SSM_GROUP, G, N))],
        out_shape=[jax.ShapeDtypeStruct((n_pow, G, N), F32)] * 2
                  + [jax.ShapeDtypeStruct((SSM_GROUP, G, N), F32)] * 2,
        name="s5_prep",
    )(lam_re, lam_im, jnp.broadcast_to(log_dt[:, None], (G, N)),
      jnp.transpose(b_re, (2, 0, 1)), jnp.transpose(b_im, (2, 0, 1)))

    bb = lambda t: _block_diag_tiles(jnp.transpose(t, (1, 0, 2)))
    wb = jnp.concatenate([bb(bb_re), bb(bb_im)], axis=-1).astype(BF16)
    cc = lambda t: _block_diag_tiles(jnp.transpose(t, (0, 2, 1)))
    wc = jnp.concatenate([cc(c_re), -cc(c_im)], axis=1).astype(BF16)
    pw = lambda t: jnp.transpose(t.reshape(n_pow, T, S), (1, 0, 2))

    y = pl.pallas_call(
        _s5_scan_kernel,
        grid=(T, L // tc),
        in_specs=[
            pl.BlockSpec((tc, LANES), lambda c, t: (t, ub + c)),
            pl.BlockSpec((None, LANES, 2 * S), lambda c, t: (c, 0, 0)),
            pl.BlockSpec((None, 2 * S, LANES), lambda c, t: (c, 0, 0)),
            pl.BlockSpec((None, n_pow, S), lambda c, t: (c, 0, 0)),
            pl.BlockSpec((None, n_pow, S), lambda c, t: (c, 0, 0)),
            pl.BlockSpec((1, LANES), lambda c, t: (0, c)),
        ],
        out_specs=pl.BlockSpec((tc, LANES), lambda c, t: (t, c)),
        out_shape=jax.ShapeDtypeStruct((L, C), F32),
        scratch_shapes=[pltpu.VMEM((tc, LANES), F32), pltpu.VMEM((tc, 2 * S), F32),
                        pltpu.VMEM((tc, LANES), F32), pltpu.VMEM((SUBLANES, 2 * S), F32)],
        compiler_params=_params("parallel", "arbitrary"),
        name="s5_scan",
    )(u_src, wb, wc, pw(p_re), pw(p_im), d_skip)

    tm = _pick(L, (512, 256, 128))
    return pl.pallas_call(
        _s5_glu_kernel,
        grid=(L // tm,),
        in_specs=[pl.BlockSpec((tm, C), lambda i: (i, 0)),
                  pl.BlockSpec((None, C, C), lambda i: (layer, 0, 0)),
                  pl.BlockSpec((1, C), lambda i: (0, 0))],
        out_specs=pl.BlockSpec((tm, C), lambda i: (i, 0)),
        out_shape=jax.ShapeDtypeStruct((L, C), F32),
        compiler_params=_params("parallel"),
        name="s5_glu",
    )(y, w_glu, b_glu)


def kernel(x, norm_mix_pre, norm_mix_post, norm_ffn_pre, norm_ffn_post, w_in, norm_out_sb, norm_out_dil, norm_out_ssm, ssm_lambda_re, ssm_lambda_im, ssm_log_dt, ssm_b_re, ssm_b_im, ssm_c_re, ssm_c_im, ssm_d, ssm_w_glu, ssm_b_glu, w_out, ffn_w_gate, ffn_w_up, ffn_w_down):
    B, L, D = x.shape
    depth = w_in.shape[0]
    ssm_c = ssm_d.shape[-1]
    sb_w = norm_out_sb.shape[-1]
    dl_w = norm_out_dil.shape[-1]
    n_sb = sb_w // HEAD_DIM
    n_dl = dl_w // HEAD_DIM
    assert w_in.shape[-1] == 3 * sb_w + 3 * dl_w + ssm_c
    big = (1024, 512, 256, 128)

    w_out_b = w_out.astype(BF16)
    w_glu_b = ssm_w_glu.astype(BF16)
    row = lambda a, l: a[l][None, :]

    outs = []
    for b in range(B):
        xb = x[b]
        h = _norm(xb, row(norm_mix_pre, 0))
        for l in range(depth):
            qkv_sb = _mm(h, w_in, l, 0, 3 * sb_w, BF16, big, (512, 384, 256, 128))
            rest = _mm(h, w_in, l, 3 * sb_w, 3 * dl_w + ssm_c, F32, big, (512, 256, 128))
            o_sb = _sb_attention(qkv_sb, n_sb)
            o_dl = _dil_attention(rest, n_dl)
            o_ssm = _s5_layer(rest, 3 * dl_w, ssm_lambda_re[l], ssm_lambda_im[l], ssm_log_dt[l],
                              ssm_b_re[l], ssm_b_im[l], ssm_c_re[l], ssm_c_im[l], row(ssm_d, l),
                              w_glu_b, row(ssm_b_glu, l), l)
            y = _out_proj(o_sb, o_dl, o_ssm, row(norm_out_sb, l), row(norm_out_dil, l), row(norm_out_ssm, l),
                          w_out_b, l)
            xb, h = _resid_norm(xb, y, row(norm_mix_post, l), row(norm_ffn_pre, l))
            f = _ffn_in(h, ffn_w_gate, ffn_w_up, l)
            y = _mm(f, ffn_w_down, l, 0, D, F32, big, (256, 128), single_buffer_a=True)
            if l + 1 < depth:
                xb, h = _resid_norm(xb, y, row(norm_ffn_post, l), row(norm_mix_pre, l + 1))
            else:
                xb = _resid_norm(xb, y, row(norm_ffn_post, l))
        outs.append(xb)
    return jnp.stack(outs)
```

```python
import functools

import jax
import jax.numpy as jnp
from jax import lax
from jax.experimental import pallas as pl
from jax.experimental.pallas import tpu as pltpu

HEAD_DIM = 128
SSM_GROUP = 16
SSM_STATE = 64
DIL_BLOCK = 128
DIL_PATTERNS = ((128, 1), (512, 4), (2048, 16))
RMS_EPS = 1e-6

LANES = 128
SUBLANES = 8
GROUPS_PER_TILE = LANES // SSM_GROUP
STATES_PER_TILE = GROUPS_PER_TILE * SSM_STATE
V7X_VMEM_LIMIT_BYTES = 56 * 1024 * 1024

SB_UNDERFLOW = 110.0
SB_HEADS_PER_STEP = 4
DIL_SUPER = DIL_BLOCK * max(d for _, d in DIL_PATTERNS)

F32 = jnp.float32
BF16 = jnp.bfloat16


def _pick(n, candidates):
    for c in candidates:
        if n % c == 0:
            return c
    raise ValueError(f"no tile in {candidates} divides {n}")


def _params(*sem):
    return pltpu.CompilerParams(dimension_semantics=sem, vmem_limit_bytes=V7X_VMEM_LIMIT_BYTES)


def _rms(x, g):
    return x * lax.rsqrt(jnp.mean(x * x, axis=-1, keepdims=True) + RMS_EPS) * g


def _dot(a, b):
    return jnp.dot(a, b, preferred_element_type=F32)


def _dot_nt(a, b):
    return lax.dot_general(a, b, (((1,), (1,)), ((), ())), preferred_element_type=F32)


def _norm_kernel(x_ref, g_ref, h_ref):
    h_ref[...] = _rms(x_ref[...], g_ref[...]).astype(h_ref.dtype)


def _norm(x, g):
    L, D = x.shape
    tm = _pick(L, (256, 128))
    row = pl.BlockSpec((tm, D), lambda i: (i, 0))
    return pl.pallas_call(
        _norm_kernel,
        grid=(L // tm,),
        in_specs=[row, pl.BlockSpec((1, D), lambda i: (0, 0))],
        out_specs=row,
        out_shape=jax.ShapeDtypeStruct((L, D), BF16),
        compiler_params=_params("parallel"),
        name="norm",
    )(x, g)


def _resid_norm_kernel(x_ref, y_ref, g_ref, o_ref):
    o_ref[...] = x_ref[...] + _rms(y_ref[...], g_ref[...])


def _resid_norm_next_kernel(x_ref, y_ref, g_ref, gn_ref, o_ref, h_ref):
    x = x_ref[...] + _rms(y_ref[...], g_ref[...])
    o_ref[...] = x
    h_ref[...] = _rms(x, gn_ref[...]).astype(h_ref.dtype)


def _resid_norm(x, y, g, g_next=None):
    L, D = x.shape
    tm = _pick(L, (256, 128))
    row = pl.BlockSpec((tm, D), lambda i: (i, 0))
    gain = pl.BlockSpec((1, D), lambda i: (0, 0))
    if g_next is None:
        return pl.pallas_call(
            _resid_norm_kernel,
            grid=(L // tm,),
            in_specs=[row, row, gain],
            out_specs=row,
            out_shape=jax.ShapeDtypeStruct((L, D), F32),
            compiler_params=_params("parallel"),
            name="resid_norm",
        )(x, y, g)
    return pl.pallas_call(
        _resid_norm_next_kernel,
        grid=(L // tm,),
        in_specs=[row, row, gain, gain],
        out_specs=[row, row],
        out_shape=[jax.ShapeDtypeStruct((L, D), F32), jax.ShapeDtypeStruct((L, D), BF16)],
        compiler_params=_params("parallel"),
        name="resid_norm_next",
    )(x, y, g, g_next)


def _mm_kernel(a_ref, w_ref, o_ref):
    o_ref[...] = _dot(a_ref[...], w_ref[...].astype(BF16)).astype(o_ref.dtype)


def _mm(a, w, layer, col0, ncols, out_dtype, tms, tns, single_buffer_a=False):
    L, K = a.shape
    tm = _pick(L, tms)
    tn = _pick(ncols, tns)
    assert col0 % tn == 0
    jb = col0 // tn
    a_mode = dict(pipeline_mode=pl.Buffered(1)) if single_buffer_a else {}
    return pl.pallas_call(
        _mm_kernel,
        grid=(L // tm, ncols // tn),
        in_specs=[
            pl.BlockSpec((tm, K), lambda i, j: (i, 0), **a_mode),
            pl.BlockSpec((None, K, tn), lambda i, j: (layer, 0, j + jb)),
        ],
        out_specs=pl.BlockSpec((tm, tn), lambda i, j: (i, j)),
        out_shape=jax.ShapeDtypeStruct((L, ncols), out_dtype),
        compiler_params=_params("parallel", "arbitrary"),
        name="mm",
    )(a, w)


def _ffn_in_kernel(h_ref, wg_ref, wu_ref, o_ref):
    h = h_ref[...]
    gate = _dot(h, wg_ref[...].astype(BF16))
    up = _dot(h, wu_ref[...].astype(BF16))
    o_ref[...] = (jax.nn.silu(gate) * up).astype(o_ref.dtype)


def _ffn_in(h, wg, wu, layer):
    L, D = h.shape
    F = wg.shape[-1]
    tm = _pick(L, (2048, 1024, 512, 256, 128))
    tn = _pick(F, (256, 128))
    wspec = pl.BlockSpec((None, D, tn), lambda i, j: (layer, 0, j))
    return pl.pallas_call(
        _ffn_in_kernel,
        grid=(L // tm, F // tn),
        in_specs=[pl.BlockSpec((tm, D), lambda i, j: (i, 0)), wspec, wspec],
        out_specs=pl.BlockSpec((tm, tn), lambda i, j: (i, j)),
        out_shape=jax.ShapeDtypeStruct((L, F), BF16),
        compiler_params=_params("parallel", "arbitrary"),
        name="ffn_in",
    )(h, wg, wu)


def _out_proj_kernel(a_ref, b_ref, c_ref, ga_ref, gb_ref, gc_ref, w_ref, o_ref, h_ref):
    @pl.when(pl.program_id(1) == 0)
    def _():
        wa = a_ref.shape[1]
        wb = b_ref.shape[1]
        h_ref[:, :wa] = _rms(a_ref[...], ga_ref[...]).astype(h_ref.dtype)
        h_ref[:, wa:wa + wb] = _rms(b_ref[...], gb_ref[...]).astype(h_ref.dtype)
        h_ref[:, wa + wb:] = _rms(c_ref[...], gc_ref[...]).astype(h_ref.dtype)

    o_ref[...] = _dot(h_ref[...], w_ref[...])


def _out_proj(a, b, c, ga, gb, gc, w, layer):
    L = a.shape[0]
    K = a.shape[1] + b.shape[1] + c.shape[1]
    N = w.shape[-1]
    tm = _pick(L, (512, 256, 128))
    tn = _pick(N, (1024, 512, 256, 128))
    rows = lambda arr: pl.BlockSpec((tm, arr.shape[1]), lambda i, j: (i, 0))
    gain = lambda arr: pl.BlockSpec((1, arr.shape[1]), lambda i, j: (0, 0))
    return pl.pallas_call(
        _out_proj_kernel,
        grid=(L // tm, N // tn),
        in_specs=[rows(a), rows(b), rows(c), gain(ga), gain(gb), gain(gc),
                  pl.BlockSpec((None, K, tn), lambda i, j: (layer, 0, j))],
        out_specs=pl.BlockSpec((tm, tn), lambda i, j: (i, j)),
        out_shape=jax.ShapeDtypeStruct((L, N), F32),
        scratch_shapes=[pltpu.VMEM((tm, K), BF16)],
        compiler_params=_params("parallel", "arbitrary"),
        name="out_proj",
    )(a, b, c, ga, gb, gc, w)


def _sb_kernel(q_ref, k_ref, v_ref, o_ref, *, tq, hp, scale):
    i = pl.program_id(1)
    heads = range(hp)
    cols = lambda h: slice(h * HEAD_DIM, (h + 1) * HEAD_DIM)
    qs = [q_ref[:, cols(h)] for h in heads]
    row = lax.broadcasted_iota(jnp.int32, (tq, tq), 0)
    col = lax.broadcasted_iota(jnp.int32, (tq, tq), 1)
    from_here = jnp.where(row >= col, 1.0, 0.0).astype(BF16)
    from_here = jnp.concatenate([from_here, from_here], axis=0)
    causal = col < row

    def blocks(off, carries, diagonal):
        zs = [_dot_nt(qs[h], k_ref[pl.ds(off, tq), cols(h)]) * scale for h in heads]
        sps, splits = [], []
        for z in zs:
            sp = jnp.maximum(z, 0.0) + jnp.log(1.0 + jnp.exp(-jnp.abs(z)))
            if diagonal:
                sp = jnp.where(causal, sp, 0.0)
            hi = sp.astype(BF16)
            lo = (sp - hi.astype(F32)).astype(BF16)
            sps.append(sp)
            splits.append(jnp.concatenate([hi, lo], axis=1))
        suffixes = [_dot(s, from_here) for s in splits]
        pvs = []
        for h in heads:
            a = jnp.exp(zs[h] - suffixes[h] - carries[h])
            if diagonal:
                a = jnp.where(causal, a, 0.0)
            pvs.append(_dot(a.astype(BF16), v_ref[pl.ds(off, tq), cols(h)]))
        carries = [carries[h] + jnp.sum(sps[h], axis=-1, keepdims=True) for h in heads]
        return pvs, carries

    def live(carries):
        return jnp.min(jnp.broadcast_to(functools.reduce(jnp.minimum, carries), (tq, LANES)))

    accs, carries = blocks(pl.multiple_of(i * tq, tq), [jnp.zeros((tq, 1), F32)] * hp, True)

    def cond(st):
        return jnp.logical_and(st[0] < i, st[1] < SB_UNDERFLOW)

    def body(st):
        s, _, accs, carries = st
        pvs, carries = blocks(pl.multiple_of((i - 1 - s) * tq, tq), carries, False)
        return s + 1, live(carries), [a + p for a, p in zip(accs, pvs)], carries

    _, _, accs, _ = lax.while_loop(cond, body, (jnp.int32(0), live(carries), accs, carries))
    for h in heads:
        o_ref[:, cols(h)] = accs[h]


def _sb_attention(qkv, n_heads):
    L = qkv.shape[0]
    tq = _pick(L, (256, 128))
    hp = _pick(n_heads, (SB_HEADS_PER_STEP, 1))
    w = hp * HEAD_DIM
    nb = n_heads // hp
    kern = functools.partial(_sb_kernel, tq=tq, hp=hp, scale=HEAD_DIM ** -0.5)
    return pl.pallas_call(
        kern,
        grid=(nb, L // tq),
        in_specs=[
            pl.BlockSpec((tq, w), lambda h, i: (i, h)),
            pl.BlockSpec((L, w), lambda h, i: (0, nb + h)),
            pl.BlockSpec((L, w), lambda h, i: (0, 2 * nb + h)),
        ],
        out_specs=pl.BlockSpec((tq, w), lambda h, i: (i, h)),
        out_shape=jax.ShapeDtypeStruct((L, n_heads * HEAD_DIM), F32),
        compiler_params=_params("parallel", "arbitrary"),
        name="sb_attention",
    )(qkv, qkv, qkv)


def _dil_kernel(q_ref, k_ref, v_ref, o_ref, qd, kd, vd, ob, mb, sb, *, scale):
    S = DIL_SUPER
    B = DIL_BLOCK
    n = pl.program_id(1)
    first = n == 0
    cur = pl.multiple_of((n % 2) * S, S)
    old = pl.multiple_of(((n + 1) % 2) * S, S)

    @pl.when(first)
    def _():
        kd[...] = jnp.zeros_like(kd)
        vd[...] = jnp.zeros_like(vd)

    for bi, (window, d) in enumerate(DIL_PATTERNS):
        per = S // d
        for r in range(d):
            src = pl.ds(r, per, stride=d) if d > 1 else pl.ds(0, per)
            qd[bi, r * per:(r + 1) * per, :] = q_ref[src, :].astype(BF16)
            kd[bi, pl.ds(cur + r * per, per), :] = k_ref[src, :].astype(BF16)
            vd[bi, pl.ds(cur + r * per, per), :] = v_ref[src, :].astype(BF16)

    qi = lax.broadcasted_iota(jnp.int32, (B, 2 * B), 0)
    ki = lax.broadcasted_iota(jnp.int32, (B, 2 * B), 1)
    upto_self = ki <= qi + B
    band = jnp.logical_and(ki >= qi, upto_self)
    band_first = jnp.logical_and(ki >= jnp.maximum(qi, jnp.where(first, B, 0)), upto_self)

    def window_of(ref, bi, r, blk, per, nblk):
        if blk > 0:
            return ref[bi, pl.ds(cur + r * per + (blk - 1) * B, 2 * B), :]
        return jnp.concatenate([ref[bi, pl.ds(old + r * per + (nblk - 1) * B, B), :],
                                ref[bi, pl.ds(cur + r * per, B), :]], axis=0)

    for bi, (window, d) in enumerate(DIL_PATTERNS):
        per = S // d
        nblk = per // B
        for r in range(d):
            for blk in range(nblk):
                q = qd[bi, r * per + blk * B:r * per + (blk + 1) * B, :]
                sc = _dot_nt(q, window_of(kd, bi, r, blk, per, nblk)) * scale
                sc = jnp.where(band_first if blk == 0 else band, sc, -jnp.inf)
                m = jnp.max(sc, axis=-1, keepdims=True)
                p = jnp.exp(sc - m)
                s = jnp.sum(p, axis=-1, keepdims=True)
                dst = pl.ds(blk * B * d + r, B, stride=d) if d > 1 else pl.ds(blk * B, B)
                ob[bi, dst, :] = _dot(p.astype(BF16), window_of(vd, bi, r, blk, per, nblk))
                mb[bi, dst, :] = jnp.broadcast_to(m, (B, LANES))
                sb[bi, dst, :] = jnp.broadcast_to(s, (B, LANES))

    n_br = len(DIL_PATTERNS)
    chunk = 256

    def combine(c, _):
        r = pl.ds(pl.multiple_of(c * chunk, chunk), chunk)
        ms = [mb[b, r, :] for b in range(n_br)]
        m_max = functools.reduce(jnp.maximum, ms)
        ws = [jnp.exp(m - m_max) for m in ms]
        num = sum(ws[b] * ob[b, r, :] for b in range(n_br))
        den = sum(ws[b] * sb[b, r, :] for b in range(n_br))
        o_ref[r, :] = num / den
        return 0

    lax.fori_loop(0, S // chunk, combine, 0)


def _dil_attention(qkv, n_heads):
    L = qkv.shape[0]
    S = DIL_SUPER
    assert L % S == 0 and all(w // d == DIL_BLOCK and S % (DIL_BLOCK * d) == 0 for w, d in DIL_PATTERNS)
    block = lambda which: pl.BlockSpec((S, HEAD_DIM), lambda h, n: (n, which * n_heads + h))
    n_br = len(DIL_PATTERNS)
    return pl.pallas_call(
        functools.partial(_dil_kernel, scale=HEAD_DIM ** -0.5),
        grid=(n_heads, L // S),
        in_specs=[block(0), block(1), block(2)],
        out_specs=pl.BlockSpec((S, HEAD_DIM), lambda h, n: (n, h)),
        out_shape=jax.ShapeDtypeStruct((L, n_heads * HEAD_DIM), F32),
        scratch_shapes=[pltpu.VMEM((n_br, S, HEAD_DIM), BF16), pltpu.VMEM((n_br, 2 * S, HEAD_DIM), BF16),
                        pltpu.VMEM((n_br, 2 * S, HEAD_DIM), BF16),
                        pltpu.VMEM((n_br, S, HEAD_DIM), F32), pltpu.VMEM((n_br, S, LANES), F32),
                        pltpu.VMEM((n_br, S, LANES), F32)],
        compiler_params=_params("arbitrary", "arbitrary"),
        name="dilated_attention",
    )(qkv, qkv, qkv)


def _s5_prep_kernel(lre_ref, lim_ref, ldt_ref, bre_ref, bim_ref, pre_ref, pim_ref, bbre_ref, bbim_ref):
    a_re = jnp.minimum(lre_ref[...], -1e-4)
    a_im = lim_ref[...]
    dt = jnp.exp(ldt_ref[...])
    mag = jnp.exp(dt * a_re)
    ang = dt * a_im
    abar_re = mag * jnp.cos(ang)
    abar_im = mag * jnp.sin(ang)
    den = a_re * a_re + a_im * a_im
    nr = abar_re - 1.0
    f_re = (nr * a_re + abar_im * a_im) / den
    f_im = (abar_im * a_re - nr * a_im) / den
    for c in range(bre_ref.shape[0]):
        br = bre_ref[c]
        bi = bim_ref[c]
        bbre_ref[c] = f_re * br - f_im * bi
        bbim_ref[c] = f_re * bi + f_im * br
    pre_ref[0] = abar_re
    pim_ref[0] = abar_im

    def power(k, p):
        pr, pi = p
        pr, pi = pr * abar_re - pi * abar_im, pr * abar_im + pi * abar_re
        pre_ref[k] = pr
        pim_ref[k] = pi
        return pr, pi

    lax.fori_loop(1, pre_ref.shape[0], power, (abar_re, abar_im))


def _s5_scan_kernel(u_ref, wb_ref, wc_ref, pre_ref, pim_ref, d_ref, y_ref, up_ref, x_ref, yp_ref, carry_ref):
    S = STATES_PER_TILE
    tc = u_ref.shape[0]
    J = tc // SUBLANES

    @pl.when(pl.program_id(1) == 0)
    def _():
        carry_ref[...] = jnp.zeros_like(carry_ref)

    for s in range(SUBLANES):
        up_ref[pl.ds(s, J, stride=SUBLANES), :] = u_ref[pl.ds(s * J, J), :]
    x_ref[...] = _dot(up_ref[...].astype(BF16), wb_ref[...])

    a_re = jnp.broadcast_to(pre_ref[0:1, :], (SUBLANES, S))
    a_im = jnp.broadcast_to(pim_ref[0:1, :], (SUBLANES, S))

    def scan(j, c):
        c_re, c_im = c
        r = pl.ds(pl.multiple_of(j * SUBLANES, SUBLANES), SUBLANES)
        c_re, c_im = (a_re * c_re - a_im * c_im + x_ref[r, :S],
                      a_re * c_im + a_im * c_re + x_ref[r, S:])
        x_ref[r, :S] = c_re
        x_ref[r, S:] = c_im
        return c_re, c_im

    zero = jnp.zeros((SUBLANES, S), F32)
    e_re, e_im = lax.fori_loop(0, J, scan, (zero, zero), unroll=2)

    aj_re = pre_ref[J - 1:J, :]
    aj_im = pim_ref[J - 1:J, :]
    g_re = carry_ref[0:1, :S]
    g_im = carry_ref[0:1, S:]
    sub = lax.broadcasted_iota(jnp.int32, (SUBLANES, S), 0)
    start_re = zero
    start_im = zero
    for s in range(SUBLANES):
        start_re = jnp.where(sub == s, g_re, start_re)
        start_im = jnp.where(sub == s, g_im, start_im)
        g_re, g_im = (e_re[s:s + 1, :] + aj_re * g_re - aj_im * g_im,
                      e_im[s:s + 1, :] + aj_re * g_im + aj_im * g_re)
    carry_ref[0:1, :S] = g_re
    carry_ref[0:1, S:] = g_im

    def fix(j, _):
        r = pl.ds(pl.multiple_of(j * SUBLANES, SUBLANES), SUBLANES)
        p_re = pre_ref[pl.ds(j, 1), :]
        p_im = pim_ref[pl.ds(j, 1), :]
        x_ref[r, :S] = x_ref[r, :S] + p_re * start_re - p_im * start_im
        x_ref[r, S:] = x_ref[r, S:] + p_re * start_im + p_im * start_re
        return 0

    lax.fori_loop(0, J, fix, 0, unroll=2)

    yp_ref[...] = _dot(x_ref[...].astype(BF16), wc_ref[...])
    for s in range(SUBLANES):
        rows = pl.ds(s * J, J)
        y_ref[rows, :] = yp_ref[pl.ds(s, J, stride=SUBLANES), :] + d_ref[...] * u_ref[rows, :]


def _s5_glu_kernel(y_ref, w_ref, b_ref, o_ref):
    g = jax.nn.gelu(y_ref[...])
    o_ref[...] = g * jax.nn.sigmoid(_dot(g.astype(BF16), w_ref[...]) + b_ref[...])


def _block_diag_tiles(m):
    G, a, b = m.shape
    T = G // GROUPS_PER_TILE
    m = m.reshape(T, GROUPS_PER_TILE, a, b)
    eye = jnp.eye(GROUPS_PER_TILE, dtype=m.dtype)
    return jnp.einsum("tgab,gh->tgahb", m, eye).reshape(T, GROUPS_PER_TILE * a, GROUPS_PER_TILE * b)


def _s5_layer(u_src, u_col0, lam_re, lam_im, log_dt, b_re, b_im, c_re, c_im, d_skip, w_glu, b_glu, layer):
    L = u_src.shape[0]
    G, N = lam_re.shape
    C = G * SSM_GROUP
    T = G // GROUPS_PER_TILE
    S = STATES_PER_TILE
    assert N == SSM_STATE and C % LANES == 0 and u_col0 % LANES == 0
    ub = u_col0 // LANES
    tc = _pick(L, (1024, 512, 256, 128))
    n_pow = tc // SUBLANES

    full = lambda shape: pl.BlockSpec(shape, lambda: (0,) * len(shape))
    p_re, p_im, bb_re, bb_im = pl.pallas_call(
        _s5_prep_kernel,
        in_specs=[full((G, N)), full((G, N)), full((G, N)), full((SSM_GROUP, G, N)), full((SSM_GROUP, G, N))],
        out_specs=[full((n_pow, G, N)), full((n_pow, G, N)),
                   full((SSM_GROUP, G, N)), full((SSM_GROUP, G, N))],
        out_shape=[jax.ShapeDtypeStruct((n_pow, G, N), F32)] * 2
                  + [jax.ShapeDtypeStruct((SSM_GROUP, G, N), F32)] * 2,
        name="s5_prep",
    )(lam_re, lam_im, jnp.broadcast_to(log_dt[:, None], (G, N)),
      jnp.transpose(b_re, (2, 0, 1)), jnp.transpose(b_im, (2, 0, 1)))

    bb = lambda t: _block_diag_tiles(jnp.transpose(t, (1, 0, 2)))
    wb = jnp.concatenate([bb(bb_re), bb(bb_im)], axis=-1).astype(BF16)
    cc = lambda t: _block_diag_tiles(jnp.transpose(t, (0, 2, 1)))
    wc = jnp.concatenate([cc(c_re), -cc(c_im)], axis=1).astype(BF16)
    pw = lambda t: jnp.transpose(t.reshape(n_pow, T, S), (1, 0, 2))

    y = pl.pallas_call(
        _s5_scan_kernel,
        grid=(T, L // tc),
        in_specs=[
            pl.BlockSpec((tc, LANES), lambda c, t: (t, ub + c)),
            pl.BlockSpec((None, LANES, 2 * S), lambda c, t: (c, 0, 0)),
            pl.BlockSpec((None, 2 * S, LANES), lambda c, t: (c, 0, 0)),
            pl.BlockSpec((None, n_pow, S), lambda c, t: (c, 0, 0)),
            pl.BlockSpec((None, n_pow, S), lambda c, t: (c, 0, 0)),
            pl.BlockSpec((1, LANES), lambda c, t: (0, c)),
        ],
        out_specs=pl.BlockSpec((tc, LANES), lambda c, t: (t, c)),
        out_shape=jax.ShapeDtypeStruct((L, C), F32),
        scratch_shapes=[pltpu.VMEM((tc, LANES), F32), pltpu.VMEM((tc, 2 * S), F32),
                        pltpu.VMEM((tc, LANES), F32), pltpu.VMEM((SUBLANES, 2 * S), F32)],
        compiler_params=_params("parallel", "arbitrary"),
        name="s5_scan",
    )(u_src, wb, wc, pw(p_re), pw(p_im), d_skip)

    tm = _pick(L, (512, 256, 128))
    return pl.pallas_call(
        _s5_glu_kernel,
        grid=(L // tm,),
        in_specs=[pl.BlockSpec((tm, C), lambda i: (i, 0)),
                  pl.BlockSpec((None, C, C), lambda i: (layer, 0, 0)),
                  pl.BlockSpec((1, C), lambda i: (0, 0))],
        out_specs=pl.BlockSpec((tm, C), lambda i: (i, 0)),
        out_shape=jax.ShapeDtypeStruct((L, C), F32),
        compiler_params=_params("parallel"),
        name="s5_glu",
    )(y, w_glu, b_glu)


def kernel(x, norm_mix_pre, norm_mix_post, norm_ffn_pre, norm_ffn_post, w_in, norm_out_sb, norm_out_dil, norm_out_ssm, ssm_lambda_re, ssm_lambda_im, ssm_log_dt, ssm_b_re, ssm_b_im, ssm_c_re, ssm_c_im, ssm_d, ssm_w_glu, ssm_b_glu, w_out, ffn_w_gate, ffn_w_up, ffn_w_down):
    B, L, D = x.shape
    depth = w_in.shape[0]
    ssm_c = ssm_d.shape[-1]
    sb_w = norm_out_sb.shape[-1]
    dl_w = norm_out_dil.shape[-1]
    n_sb = sb_w // HEAD_DIM
    n_dl = dl_w // HEAD_DIM
    assert w_in.shape[-1] == 3 * sb_w + 3 * dl_w + ssm_c
    big = (1024, 512, 256, 128)

    w_out_b = w_out.astype(BF16)
    w_glu_b = ssm_w_glu.astype(BF16)
    row = lambda a, l: a[l][None, :]

    outs = []
    for b in range(B):
        xb = x[b]
        h = _norm(xb, row(norm_mix_pre, 0))
        for l in range(depth):
            qkv_sb = _mm(h, w_in, l, 0, 3 * sb_w, BF16, big, (768, 512, 384, 256, 128))
            rest = _mm(h, w_in, l, 3 * sb_w, 3 * dl_w + ssm_c, F32, big, (512, 256, 128))
            o_sb = _sb_attention(qkv_sb, n_sb)
            o_dl = _dil_attention(rest, n_dl)
            o_ssm = _s5_layer(rest, 3 * dl_w, ssm_lambda_re[l], ssm_lambda_im[l], ssm_log_dt[l],
                              ssm_b_re[l], ssm_b_im[l], ssm_c_re[l], ssm_c_im[l], row(ssm_d, l),
                              w_glu_b, row(ssm_b_glu, l), l)
            y = _out_proj(o_sb, o_dl, o_ssm, row(norm_out_sb, l), row(norm_out_dil, l), row(norm_out_ssm, l),
                          w_out_b, l)
            xb, h = _resid_norm(xb, y, row(norm_mix_post, l), row(norm_ffn_pre, l))
            f = _ffn_in(h, ffn_w_gate, ffn_w_up, l)
            y = _mm(f, ffn_w_down, l, 0, D, F32, big, (256, 128), single_buffer_a=True)
            if l + 1 < depth:
                xb, h = _resid_norm(xb, y, row(norm_ffn_post, l), row(norm_mix_pre, l + 1))
            else:
                xb = _resid_norm(xb, y, row(norm_ffn_post, l))
        outs.append(xb)
    return jnp.stack(outs)
```

```python
import functools

import jax
import jax.numpy as jnp
from jax import lax
from jax.experimental import pallas as pl
from jax.experimental.pallas import tpu as pltpu

HEAD_DIM = 128
SSM_GROUP = 16
SSM_STATE = 64
DIL_BLOCK = 128
DIL_PATTERNS = ((128, 1), (512, 4), (2048, 16))
RMS_EPS = 1e-6

LANES = 128
SUBLANES = 8
GROUPS_PER_TILE = LANES // SSM_GROUP
STATES_PER_TILE = GROUPS_PER_TILE * SSM_STATE
V7X_VMEM_LIMIT_BYTES = 56 * 1024 * 1024

SB_UNDERFLOW = 110.0
SB_HEADS_PER_STEP = 4
DIL_SUPER = DIL_BLOCK * max(d for _, d in DIL_PATTERNS)

F32 = jnp.float32
BF16 = jnp.bfloat16


def _pick(n, candidates):
    for c in candidates:
        if n % c == 0:
            return c
    raise ValueError(f"no tile in {candidates} divides {n}")


def _params(*sem):
    return pltpu.CompilerParams(dimension_semantics=sem, vmem_limit_bytes=V7X_VMEM_LIMIT_BYTES)


def _rms(x, g):
    return x * lax.rsqrt(jnp.mean(x * x, axis=-1, keepdims=True) + RMS_EPS) * g


def _dot(a, b):
    return jnp.dot(a, b, preferred_element_type=F32)


def _dot_nt(a, b):
    return lax.dot_general(a, b, (((1,), (1,)), ((), ())), preferred_element_type=F32)


def _norm_kernel(x_ref, g_ref, h_ref):
    h_ref[...] = _rms(x_ref[...], g_ref[...]).astype(h_ref.dtype)


def _norm(x, g):
    L, D = x.shape
    tm = _pick(L, (256, 128))
    row = pl.BlockSpec((tm, D), lambda i: (i, 0))
    return pl.pallas_call(
        _norm_kernel,
        grid=(L // tm,),
        in_specs=[row, pl.BlockSpec((1, D), lambda i: (0, 0))],
        out_specs=row,
        out_shape=jax.ShapeDtypeStruct((L, D), BF16),
        compiler_params=_params("parallel"),
        name="norm",
    )(x, g)


def _resid_norm_kernel(x_ref, y_ref, g_ref, o_ref):
    o_ref[...] = x_ref[...] + _rms(y_ref[...], g_ref[...])


def _resid_norm_next_kernel(x_ref, y_ref, g_ref, gn_ref, o_ref, h_ref):
    x = x_ref[...] + _rms(y_ref[...], g_ref[...])
    o_ref[...] = x
    h_ref[...] = _rms(x, gn_ref[...]).astype(h_ref.dtype)


def _resid_norm(x, y, g, g_next=None):
    L, D = x.shape
    tm = _pick(L, (256, 128))
    row = pl.BlockSpec((tm, D), lambda i: (i, 0))
    gain = pl.BlockSpec((1, D), lambda i: (0, 0))
    if g_next is None:
        return pl.pallas_call(
            _resid_norm_kernel,
            grid=(L // tm,),
            in_specs=[row, row, gain],
            out_specs=row,
            out_shape=jax.ShapeDtypeStruct((L, D), F32),
            compiler_params=_params("parallel"),
            name="resid_norm",
        )(x, y, g)
    return pl.pallas_call(
        _resid_norm_next_kernel,
        grid=(L // tm,),
        in_specs=[row, row, gain, gain],
        out_specs=[row, row],
        out_shape=[jax.ShapeDtypeStruct((L, D), F32), jax.ShapeDtypeStruct((L, D), BF16)],
        compiler_params=_params("parallel"),
        name="resid_norm_next",
    )(x, y, g, g_next)


def _mm_kernel(a_ref, w_ref, o_ref):
    o_ref[...] = _dot(a_ref[...], w_ref[...].astype(BF16)).astype(o_ref.dtype)


def _mm(a, w, layer, col0, ncols, out_dtype, tms, tns, single_buffer_a=False):
    L, K = a.shape
    tm = _pick(L, tms)
    tn = _pick(ncols, tns)
    assert col0 % tn == 0
    jb = col0 // tn
    a_mode = dict(pipeline_mode=pl.Buffered(1)) if single_buffer_a else {}
    return pl.pallas_call(
        _mm_kernel,
        grid=(L // tm, ncols // tn),
        in_specs=[
            pl.BlockSpec((tm, K), lambda i, j: (i, 0), **a_mode),
            pl.BlockSpec((None, K, tn), lambda i, j: (layer, 0, j + jb)),
        ],
        out_specs=pl.BlockSpec((tm, tn), lambda i, j: (i, j)),
        out_shape=jax.ShapeDtypeStruct((L, ncols), out_dtype),
        compiler_params=_params("parallel", "arbitrary"),
        name="mm",
    )(a, w)


def _ffn_in_kernel(h_ref, wg_ref, wu_ref, o_ref):
    h = h_ref[...]
    gate = _dot(h, wg_ref[...].astype(BF16))
    up = _dot(h, wu_ref[...].astype(BF16))
    o_ref[...] = (jax.nn.silu(gate) * up).astype(o_ref.dtype)


def _ffn_in(h, wg, wu, layer):
    L, D = h.shape
    F = wg.shape[-1]
    tm = _pick(L, (2048, 1024, 512, 256, 128))
    tn = _pick(F, (256, 128))
    wspec = pl.BlockSpec((None, D, tn), lambda i, j: (layer, 0, j))
    return pl.pallas_call(
        _ffn_in_kernel,
        grid=(L // tm, F // tn),
        in_specs=[pl.BlockSpec((tm, D), lambda i, j: (i, 0)), wspec, wspec],
        out_specs=pl.BlockSpec((tm, tn), lambda i, j: (i, j)),
        out_shape=jax.ShapeDtypeStruct((L, F), BF16),
        compiler_params=_params("parallel", "arbitrary"),
        name="ffn_in",
    )(h, wg, wu)


def _out_proj_kernel(a_ref, b_ref, c_ref, ga_ref, gb_ref, gc_ref, w_ref, o_ref, h_ref):
    @pl.when(pl.program_id(1) == 0)
    def _():
        wa = a_ref.shape[1]
        wb = b_ref.shape[1]
        h_ref[:, :wa] = _rms(a_ref[...], ga_ref[...]).astype(h_ref.dtype)
        h_ref[:, wa:wa + wb] = _rms(b_ref[...], gb_ref[...]).astype(h_ref.dtype)
        h_ref[:, wa + wb:] = _rms(c_ref[...], gc_ref[...]).astype(h_ref.dtype)

    o_ref[...] = _dot(h_ref[...], w_ref[...])


def _out_proj(a, b, c, ga, gb, gc, w, layer):
    L = a.shape[0]
    K = a.shape[1] + b.shape[1] + c.shape[1]
    N = w.shape[-1]
    tm = _pick(L, (512, 256, 128))
    tn = _pick(N, (1024, 512, 256, 128))
    rows = lambda arr: pl.BlockSpec((tm, arr.shape[1]), lambda i, j: (i, 0))
    gain = lambda arr: pl.BlockSpec((1, arr.shape[1]), lambda i, j: (0, 0))
    return pl.pallas_call(
        _out_proj_kernel,
        grid=(L // tm, N // tn),
        in_specs=[rows(a), rows(b), rows(c), gain(ga), gain(gb), gain(gc),
                  pl.BlockSpec((None, K, tn), lambda i, j: (layer, 0, j))],
        out_specs=pl.BlockSpec((tm, tn), lambda i, j: (i, j)),
        out_shape=jax.ShapeDtypeStruct((L, N), F32),
        scratch_shapes=[pltpu.VMEM((tm, K), BF16)],
        compiler_params=_params("parallel", "arbitrary"),
        name="out_proj",
    )(a, b, c, ga, gb, gc, w)


def _sb_kernel(q_ref, k_ref, v_ref, o_ref, *, tq, hp, scale):
    i = pl.program_id(1)
    heads = range(hp)
    cols = lambda h: slice(h * HEAD_DIM, (h + 1) * HEAD_DIM)
    qs = [q_ref[:, cols(h)] for h in heads]
    row = lax.broadcasted_iota(jnp.int32, (tq, tq), 0)
    col = lax.broadcasted_iota(jnp.int32, (tq, tq), 1)
    from_here = jnp.where(row >= col, 1.0, 0.0).astype(BF16)
    from_here = jnp.concatenate([from_here, from_here], axis=0)
    causal = col < row

    def blocks(off, carries, diagonal):
        zs = [_dot_nt(qs[h], k_ref[pl.ds(off, tq), cols(h)]) * scale for h in heads]
        sps, splits = [], []
        for z in zs:
            sp = jnp.maximum(z, 0.0) + jnp.log(1.0 + jnp.exp(-jnp.abs(z)))
            if diagonal:
                sp = jnp.where(causal, sp, 0.0)
            hi = sp.astype(BF16)
            lo = (sp - hi.astype(F32)).astype(BF16)
            sps.append(sp)
            splits.append(jnp.concatenate([hi, lo], axis=1))
        suffixes = [_dot(s, from_here) for s in splits]
        pvs = []
        for h in heads:
            a = jnp.exp(zs[h] - suffixes[h] - carries[h])
            if diagonal:
                a = jnp.where(causal, a, 0.0)
            pvs.append(_dot(a.astype(BF16), v_ref[pl.ds(off, tq), cols(h)]))
        carries = [carries[h] + jnp.sum(sps[h], axis=-1, keepdims=True) for h in heads]
        return pvs, carries

    def live(carries):
        return jnp.min(jnp.broadcast_to(functools.reduce(jnp.minimum, carries), (tq, LANES)))

    accs, carries = blocks(pl.multiple_of(i * tq, tq), [jnp.zeros((tq, 1), F32)] * hp, True)

    def cond(st):
        return jnp.logical_and(st[0] < i, st[1] < SB_UNDERFLOW)

    def body(st):
        s, _, accs, carries = st
        pvs, carries = blocks(pl.multiple_of((i - 1 - s) * tq, tq), carries, False)
        return s + 1, live(carries), [a + p for a, p in zip(accs, pvs)], carries

    _, _, accs, _ = lax.while_loop(cond, body, (jnp.int32(0), live(carries), accs, carries))
    for h in heads:
        o_ref[:, cols(h)] = accs[h]


def _sb_attention(qkv, n_heads):
    L = qkv.shape[0]
    tq = _pick(L, (256, 128))
    hp = _pick(n_heads, (SB_HEADS_PER_STEP, 1))
    w = hp * HEAD_DIM
    nb = n_heads // hp
    kern = functools.partial(_sb_kernel, tq=tq, hp=hp, scale=HEAD_DIM ** -0.5)
    return pl.pallas_call(
        kern,
        grid=(nb, L // tq),
        in_specs=[
            pl.BlockSpec((tq, w), lambda h, i: (i, h)),
            pl.BlockSpec((L, w), lambda h, i: (0, nb + h)),
            pl.BlockSpec((L, w), lambda h, i: (0, 2 * nb + h)),
        ],
        out_specs=pl.BlockSpec((tq, w), lambda h, i: (i, h)),
        out_shape=jax.ShapeDtypeStruct((L, n_heads * HEAD_DIM), F32),
        compiler_params=_params("parallel", "arbitrary"),
        name="sb_attention",
    )(qkv, qkv, qkv)


def _dil_kernel(q_ref, k_ref, v_ref, o_ref, qd, kd, vd, ob, mb, sb, *, scale):
    S = DIL_SUPER
    B = DIL_BLOCK
    n = pl.program_id(1)
    first = n == 0
    cur = pl.multiple_of((n % 2) * S, S)
    old = pl.multiple_of(((n + 1) % 2) * S, S)

    @pl.when(first)
    def _():
        kd[...] = jnp.zeros_like(kd)
        vd[...] = jnp.zeros_like(vd)

    for bi, (window, d) in enumerate(DIL_PATTERNS):
        per = S // d
        for r in range(d):
            src = pl.ds(r, per, stride=d) if d > 1 else pl.ds(0, per)
            qd[bi, r * per:(r + 1) * per, :] = q_ref[src, :].astype(BF16)
            kd[bi, pl.ds(cur + r * per, per), :] = k_ref[src, :].astype(BF16)
            vd[bi, pl.ds(cur + r * per, per), :] = v_ref[src, :].astype(BF16)

    qi = lax.broadcasted_iota(jnp.int32, (B, 2 * B), 0)
    ki = lax.broadcasted_iota(jnp.int32, (B, 2 * B), 1)
    upto_self = ki <= qi + B
    band = jnp.logical_and(ki >= qi, upto_self)
    band_first = jnp.logical_and(ki >= jnp.maximum(qi, jnp.where(first, B, 0)), upto_self)

    def window_of(ref, bi, r, blk, per, nblk):
        if blk > 0:
            return ref[bi, pl.ds(cur + r * per + (blk - 1) * B, 2 * B), :]
        return jnp.concatenate([ref[bi, pl.ds(old + r * per + (nblk - 1) * B, B), :],
                                ref[bi, pl.ds(cur + r * per, B), :]], axis=0)

    for bi, (window, d) in enumerate(DIL_PATTERNS):
        per = S // d
        nblk = per // B
        for r in range(d):
            for blk in range(nblk):
                q = qd[bi, r * per + blk * B:r * per + (blk + 1) * B, :]
                sc = _dot_nt(q, window_of(kd, bi, r, blk, per, nblk)) * scale
                sc = jnp.where(band_first if blk == 0 else band, sc, -jnp.inf)
                m = jnp.max(sc, axis=-1, keepdims=True)
                p = jnp.exp(sc - m)
                s = jnp.sum(p, axis=-1, keepdims=True)
                dst = pl.ds(blk * B * d + r, B, stride=d) if d > 1 else pl.ds(blk * B, B)
                ob[bi, dst, :] = _dot(p.astype(BF16), window_of(vd, bi, r, blk, per, nblk))
                mb[bi, dst, :] = jnp.broadcast_to(m, (B, LANES))
                sb[bi, dst, :] = jnp.broadcast_to(s, (B, LANES))

    n_br = len(DIL_PATTERNS)
    chunk = 256

    def combine(c, _):
        r = pl.ds(pl.multiple_of(c * chunk, chunk), chunk)
        ms = [mb[b, r, :] for b in range(n_br)]
        m_max = functools.reduce(jnp.maximum, ms)
        ws = [jnp.exp(m - m_max) for m in ms]
        num = sum(ws[b] * ob[b, r, :] for b in range(n_br))
        den = sum(ws[b] * sb[b, r, :] for b in range(n_br))
        o_ref[r, :] = num / den
        return 0

    lax.fori_loop(0, S // chunk, combine, 0)


def _dil_attention(qkv, n_heads):
    L = qkv.shape[0]
    S = DIL_SUPER
    assert L % S == 0 and all(w // d == DIL_BLOCK and S % (DIL_BLOCK * d) == 0 for w, d in DIL_PATTERNS)
    block = lambda which: pl.BlockSpec((S, HEAD_DIM), lambda h, n: (n, which * n_heads + h))
    n_br = len(DIL_PATTERNS)
    return pl.pallas_call(
        functools.partial(_dil_kernel, scale=HEAD_DIM ** -0.5),
        grid=(n_heads, L // S),
        in_specs=[block(0), block(1), block(2)],
        out_specs=pl.BlockSpec((S, HEAD_DIM), lambda h, n: (n, h)),
        out_shape=jax.ShapeDtypeStruct((L, n_heads * HEAD_DIM), F32),
        scratch_shapes=[pltpu.VMEM((n_br, S, HEAD_DIM), BF16), pltpu.VMEM((n_br, 2 * S, HEAD_DIM), BF16),
                        pltpu.VMEM((n_br, 2 * S, HEAD_DIM), BF16),
                        pltpu.VMEM((n_br, S, HEAD_DIM), F32), pltpu.VMEM((n_br, S, LANES), F32),
                        pltpu.VMEM((n_br, S, LANES), F32)],
        compiler_params=_params("arbitrary", "arbitrary"),
        name="dilated_attention",
    )(qkv, qkv, qkv)


def _s5_prep_kernel(lre_ref, lim_ref, ldt_ref, bre_ref, bim_ref, pre_ref, pim_ref, bbre_ref, bbim_ref):
    a_re = jnp.minimum(lre_ref[...], -1e-4)
    a_im = lim_ref[...]
    dt = jnp.exp(ldt_ref[...])
    mag = jnp.exp(dt * a_re)
    ang = dt * a_im
    abar_re = mag * jnp.cos(ang)
    abar_im = mag * jnp.sin(ang)
    den = a_re * a_re + a_im * a_im
    nr = abar_re - 1.0
    f_re = (nr * a_re + abar_im * a_im) / den
    f_im = (abar_im * a_re - nr * a_im) / den
    for c in range(bre_ref.shape[0]):
        br = bre_ref[c]
        bi = bim_ref[c]
        bbre_ref[c] = f_re * br - f_im * bi
        bbim_ref[c] = f_re * bi + f_im * br
    pre_ref[0] = abar_re
    pim_ref[0] = abar_im

    def power(k, p):
        pr, pi = p
        pr, pi = pr * abar_re - pi * abar_im, pr * abar_im + pi * abar_re
        pre_ref[k] = pr
        pim_ref[k] = pi
        return pr, pi

    lax.fori_loop(1, pre_ref.shape[0], power, (abar_re, abar_im))


def _s5_scan_kernel(u_ref, wb_ref, wc_ref, pre_ref, pim_ref, d_ref, a_ref, w_ref, y_ref, o_ref,
                    up_ref, x_ref, yp_ref, carry_ref, *, n_scan, steps_per_tile):
    S = STATES_PER_TILE
    tc = u_ref.shape[0]
    J = tc // SUBLANES
    step = pl.program_id(0)

    def side_tile():
        o_ref[...] = _dot(a_ref[...], w_ref[...].astype(BF16)).astype(o_ref.dtype)

    @pl.when(jnp.logical_and(step < n_scan, step % steps_per_tile == 0))
    def _():
        carry_ref[...] = jnp.zeros_like(carry_ref)

    @pl.when(step >= n_scan)
    def _():
        side_tile()

    @pl.when(step < n_scan)
    def _():
        for s in range(SUBLANES):
            up_ref[pl.ds(s, J, stride=SUBLANES), :] = u_ref[pl.ds(s * J, J), :]
        x_ref[...] = _dot(up_ref[...].astype(BF16), wb_ref[...])
        side_tile()

        a_re = jnp.broadcast_to(pre_ref[0:1, :], (SUBLANES, S))
        a_im = jnp.broadcast_to(pim_ref[0:1, :], (SUBLANES, S))
        zero = jnp.zeros((SUBLANES, S), F32)
        e_re, e_im = zero, zero
        for j in range(J):
            r = slice(j * SUBLANES, (j + 1) * SUBLANES)
            e_re, e_im = (a_re * e_re - a_im * e_im + x_ref[r, :S],
                          a_re * e_im + a_im * e_re + x_ref[r, S:])
            x_ref[r, :S] = e_re
            x_ref[r, S:] = e_im

        aj_re = pre_ref[J - 1:J, :]
        aj_im = pim_ref[J - 1:J, :]
        g_re = carry_ref[0:1, :S]
        g_im = carry_ref[0:1, S:]
        sub = lax.broadcasted_iota(jnp.int32, (SUBLANES, S), 0)
        start_re = zero
        start_im = zero
        for s in range(SUBLANES):
            start_re = jnp.where(sub == s, g_re, start_re)
            start_im = jnp.where(sub == s, g_im, start_im)
            g_re, g_im = (e_re[s:s + 1, :] + aj_re * g_re - aj_im * g_im,
                          e_im[s:s + 1, :] + aj_re * g_im + aj_im * g_re)
        carry_ref[0:1, :S] = g_re
        carry_ref[0:1, S:] = g_im

        for j in range(J):
            r = slice(j * SUBLANES, (j + 1) * SUBLANES)
            p_re = pre_ref[j:j + 1, :]
            p_im = pim_ref[j:j + 1, :]
            x_ref[r, :S] = x_ref[r, :S] + p_re * start_re - p_im * start_im
            x_ref[r, S:] = x_ref[r, S:] + p_re * start_im + p_im * start_re

        yp_ref[...] = _dot(x_ref[...].astype(BF16), wc_ref[...])
        for s in range(SUBLANES):
            rows = pl.ds(s * J, J)
            y_ref[rows, :] = yp_ref[pl.ds(s, J, stride=SUBLANES), :] + d_ref[...] * u_ref[rows, :]


def _s5_glu_kernel(y_ref, w_ref, b_ref, o_ref):
    g = jax.nn.gelu(y_ref[...])
    o_ref[...] = g * jax.nn.sigmoid(_dot(g.astype(BF16), w_ref[...]) + b_ref[...])


def _block_diag_tiles(m):
    G, a, b = m.shape
    T = G // GROUPS_PER_TILE
    m = m.reshape(T, GROUPS_PER_TILE, a, b)
    eye = jnp.eye(GROUPS_PER_TILE, dtype=m.dtype)
    return jnp.einsum("tgab,gh->tgahb", m, eye).reshape(T, GROUPS_PER_TILE * a, GROUPS_PER_TILE * b)


def _s5_layer(u_src, u_col0, lam_re, lam_im, log_dt, b_re, b_im, c_re, c_im, d_skip, w_glu, b_glu, layer,
              side_a, side_w, side_cols):
    L = u_src.shape[0]
    G, N = lam_re.shape
    C = G * SSM_GROUP
    T = G // GROUPS_PER_TILE
    S = STATES_PER_TILE
    assert N == SSM_STATE and C % LANES == 0 and u_col0 % LANES == 0
    ub = u_col0 // LANES
    tc = _pick(L, (1024, 512, 256, 128))
    n_pow = tc // SUBLANES

    full = lambda shape: pl.BlockSpec(shape, lambda: (0,) * len(shape))
    p_re, p_im, bb_re, bb_im = pl.pallas_call(
        _s5_prep_kernel,
        in_specs=[full((G, N)), full((G, N)), full((G, N)), full((SSM_GROUP, G, N)), full((SSM_GROUP, G, N))],
        out_specs=[full((n_pow, G, N)), full((n_pow, G, N)),
                   full((SSM_GROUP, G, N)), full((SSM_GROUP, G, N))],
        out_shape=[jax.ShapeDtypeStruct((n_pow, G, N), F32)] * 2
                  + [jax.ShapeDtypeStruct((SSM_GROUP, G, N), F32)] * 2,
        name="s5_prep",
    )(lam_re, lam_im, jnp.broadcast_to(log_dt[:, None], (G, N)),
      jnp.transpose(b_re, (2, 0, 1)), jnp.transpose(b_im, (2, 0, 1)))

    bb = lambda t: _block_diag_tiles(jnp.transpose(t, (1, 0, 2)))
    wb = jnp.concatenate([bb(bb_re), bb(bb_im)], axis=-1).astype(BF16)
    cc = lambda t: _block_diag_tiles(jnp.transpose(t, (0, 2, 1)))
    wc = jnp.concatenate([cc(c_re), -cc(c_im)], axis=1).astype(BF16)
    pw = lambda t: jnp.transpose(t.reshape(n_pow, T, S), (1, 0, 2))

    nt = L // tc
    n_scan = T * nt
    K = side_a.shape[1]
    side_tn = _pick(side_cols, (512, 256, 128))
    nj = side_cols // side_tn
    n_side = (L // tc) * nj
    assert side_a.shape[0] == L and n_side >= n_scan
    ct = lambda s: jnp.minimum(s, n_scan - 1) // nt
    tt = lambda s: jnp.minimum(s, n_scan - 1) % nt
    y, side = pl.pallas_call(
        functools.partial(_s5_scan_kernel, n_scan=n_scan, steps_per_tile=nt),
        grid=(n_side,),
        in_specs=[
            pl.BlockSpec((tc, LANES), lambda s: (tt(s), ub + ct(s))),
            pl.BlockSpec((None, LANES, 2 * S), lambda s: (ct(s), 0, 0)),
            pl.BlockSpec((None, 2 * S, LANES), lambda s: (ct(s), 0, 0)),
            pl.BlockSpec((None, n_pow, S), lambda s: (ct(s), 0, 0)),
            pl.BlockSpec((None, n_pow, S), lambda s: (ct(s), 0, 0)),
            pl.BlockSpec((1, LANES), lambda s: (0, ct(s))),
            pl.BlockSpec((tc, K), lambda s: (s // nj, 0)),
            pl.BlockSpec((None, K, side_tn), lambda s: (layer, 0, s % nj)),
        ],
        out_specs=[pl.BlockSpec((tc, LANES), lambda s: (tt(s), ct(s))),
                   pl.BlockSpec((tc, side_tn), lambda s: (s // nj, s % nj))],
        out_shape=[jax.ShapeDtypeStruct((L, C), F32), jax.ShapeDtypeStruct((L, side_cols), BF16)],
        scratch_shapes=[pltpu.VMEM((tc, LANES), F32), pltpu.VMEM((tc, 2 * S), F32),
                        pltpu.VMEM((tc, LANES), F32), pltpu.VMEM((SUBLANES, 2 * S), F32)],
        compiler_params=_params("arbitrary"),
        name="s5_scan",
    )(u_src, wb, wc, pw(p_re), pw(p_im), d_skip, side_a, side_w)

    tm = _pick(L, (512, 256, 128))
    return pl.pallas_call(
        _s5_glu_kernel,
        grid=(L // tm,),
        in_specs=[pl.BlockSpec((tm, C), lambda i: (i, 0)),
                  pl.BlockSpec((None, C, C), lambda i: (layer, 0, 0)),
                  pl.BlockSpec((1, C), lambda i: (0, 0))],
        out_specs=pl.BlockSpec((tm, C), lambda i: (i, 0)),
        out_shape=jax.ShapeDtypeStruct((L, C), F32),
        compiler_params=_params("parallel"),
        name="s5_glu",
    )(y, w_glu, b_glu), side


def kernel(x, norm_mix_pre, norm_mix_post, norm_ffn_pre, norm_ffn_post, w_in, norm_out_sb, norm_out_dil, norm_out_ssm, ssm_lambda_re, ssm_lambda_im, ssm_log_dt, ssm_b_re, ssm_b_im, ssm_c_re, ssm_c_im, ssm_d, ssm_w_glu, ssm_b_glu, w_out, ffn_w_gate, ffn_w_up, ffn_w_down):
    B, L, D = x.shape
    depth = w_in.shape[0]
    ssm_c = ssm_d.shape[-1]
    sb_w = norm_out_sb.shape[-1]
    dl_w = norm_out_dil.shape[-1]
    n_sb = sb_w // HEAD_DIM
    n_dl = dl_w // HEAD_DIM
    assert w_in.shape[-1] == 3 * sb_w + 3 * dl_w + ssm_c
    big = (1024, 512, 256, 128)

    w_out_b = w_out.astype(BF16)
    w_glu_b = ssm_w_glu.astype(BF16)
    row = lambda a, l: a[l][None, :]

    outs = []
    for b in range(B):
        xb = x[b]
        h = _norm(xb, row(norm_mix_pre, 0))
        for l in range(depth):
            rest = _mm(h, w_in, l, 3 * sb_w, 3 * dl_w + ssm_c, F32, big, (512, 256, 128))
            o_ssm, qkv_sb = _s5_layer(rest, 3 * dl_w, ssm_lambda_re[l], ssm_lambda_im[l], ssm_log_dt[l],
                                      ssm_b_re[l], ssm_b_im[l], ssm_c_re[l], ssm_c_im[l], row(ssm_d, l),
                                      w_glu_b, row(ssm_b_glu, l), l, h, w_in, 3 * sb_w)
            o_sb = _sb_attention(qkv_sb, n_sb)
            o_dl = _dil_attention(rest, n_dl)
            y = _out_proj(o_sb, o_dl, o_ssm, row(norm_out_sb, l), row(norm_out_dil, l), row(norm_out_ssm, l),
                          w_out_b, l)
            xb, h = _resid_norm(xb, y, row(norm_mix_post, l), row(norm_ffn_pre, l))
            f = _ffn_in(h, ffn_w_gate, ffn_w_up, l)
            y = _mm(f, ffn_w_down, l, 0, D, F32, big, (256, 128), single_buffer_a=True)
            if l + 1 < depth:
                xb, h = _resid_norm(xb, y, row(norm_ffn_post, l), row(norm_mix_pre, l + 1))
            else:
                xb = _resid_norm(xb, y, row(norm_ffn_post, l))
        outs.append(xb)
    return jnp.stack(outs)
```

```python
import functools

import jax
import jax.numpy as jnp
from jax import lax
from jax.experimental import pallas as pl
from jax.experimental.pallas import tpu as pltpu

HEAD_DIM = 128
SSM_GROUP = 16
SSM_STATE = 64
DIL_BLOCK = 128
DIL_PATTERNS = ((128, 1), (512, 4), (2048, 16))
RMS_EPS = 1e-6

LANES = 128
SUBLANES = 8
GROUPS_PER_TILE = LANES // SSM_GROUP
STATES_PER_TILE = GROUPS_PER_TILE * SSM_STATE
V7X_VMEM_LIMIT_BYTES = 56 * 1024 * 1024

SB_UNDERFLOW = 110.0
SB_HEADS_PER_STEP = 4
DIL_SUPER = DIL_BLOCK * max(d for _, d in DIL_PATTERNS)

F32 = jnp.float32
BF16 = jnp.bfloat16


def _pick(n, candidates):
    for c in candidates:
        if n % c == 0:
            return c
    raise ValueError(f"no tile in {candidates} divides {n}")


def _params(*sem):
    return pltpu.CompilerParams(dimension_semantics=sem, vmem_limit_bytes=V7X_VMEM_LIMIT_BYTES)


def _rms(x, g):
    return x * lax.rsqrt(jnp.mean(x * x, axis=-1, keepdims=True) + RMS_EPS) * g


def _dot(a, b):
    return jnp.dot(a, b, preferred_element_type=F32)


def _dot_nt(a, b):
    return lax.dot_general(a, b, (((1,), (1,)), ((), ())), preferred_element_type=F32)


def _norm_kernel(x_ref, g_ref, h_ref):
    h_ref[...] = _rms(x_ref[...], g_ref[...]).astype(h_ref.dtype)


def _norm(x, g):
    L, D = x.shape
    tm = _pick(L, (256, 128))
    row = pl.BlockSpec((tm, D), lambda i: (i, 0))
    return pl.pallas_call(
        _norm_kernel,
        grid=(L // tm,),
        in_specs=[row, pl.BlockSpec((1, D), lambda i: (0, 0))],
        out_specs=row,
        out_shape=jax.ShapeDtypeStruct((L, D), BF16),
        compiler_params=_params("parallel"),
        name="norm",
    )(x, g)


def _resid_norm_kernel(x_ref, y_ref, g_ref, o_ref):
    o_ref[...] = x_ref[...] + _rms(y_ref[...], g_ref[...])


def _resid_norm_next_kernel(x_ref, y_ref, g_ref, gn_ref, o_ref, h_ref):
    x = x_ref[...] + _rms(y_ref[...], g_ref[...])
    o_ref[...] = x
    h_ref[...] = _rms(x, gn_ref[...]).astype(h_ref.dtype)


def _resid_norm(x, y, g, g_next=None):
    L, D = x.shape
    tm = _pick(L, (256, 128))
    row = pl.BlockSpec((tm, D), lambda i: (i, 0))
    gain = pl.BlockSpec((1, D), lambda i: (0, 0))
    if g_next is None:
        return pl.pallas_call(
            _resid_norm_kernel,
            grid=(L // tm,),
            in_specs=[row, row, gain],
            out_specs=row,
            out_shape=jax.ShapeDtypeStruct((L, D), F32),
            compiler_params=_params("parallel"),
            name="resid_norm",
        )(x, y, g)
    return pl.pallas_call(
        _resid_norm_next_kernel,
        grid=(L // tm,),
        in_specs=[row, row, gain, gain],
        out_specs=[row, row],
        out_shape=[jax.ShapeDtypeStruct((L, D), F32), jax.ShapeDtypeStruct((L, D), BF16)],
        compiler_params=_params("parallel"),
        name="resid_norm_next",
    )(x, y, g, g_next)


def _mm_kernel(a_ref, w_ref, o_ref):
    o_ref[...] = _dot(a_ref[...], w_ref[...].astype(BF16)).astype(o_ref.dtype)


def _mm(a, w, layer, col0, ncols, out_dtype, tms, tns, single_buffer_a=False):
    L, K = a.shape
    tm = _pick(L, tms)
    tn = _pick(ncols, tns)
    assert col0 % tn == 0
    jb = col0 // tn
    a_mode = dict(pipeline_mode=pl.Buffered(1)) if single_buffer_a else {}
    return pl.pallas_call(
        _mm_kernel,
        grid=(L // tm, ncols // tn),
        in_specs=[
            pl.BlockSpec((tm, K), lambda i, j: (i, 0), **a_mode),
            pl.BlockSpec((None, K, tn), lambda i, j: (layer, 0, j + jb)),
        ],
        out_specs=pl.BlockSpec((tm, tn), lambda i, j: (i, j)),
        out_shape=jax.ShapeDtypeStruct((L, ncols), out_dtype),
        compiler_params=_params("parallel", "arbitrary"),
        name="mm",
    )(a, w)


def _ffn_in_kernel(h_ref, wg_ref, wu_ref, o_ref):
    h = h_ref[...]
    gate = _dot(h, wg_ref[...].astype(BF16))
    up = _dot(h, wu_ref[...].astype(BF16))
    o_ref[...] = (jax.nn.silu(gate) * up).astype(o_ref.dtype)


def _ffn_in(h, wg, wu, layer):
    L, D = h.shape
    F = wg.shape[-1]
    tm = _pick(L, (2048, 1024, 512, 256, 128))
    tn = _pick(F, (256, 128))
    wspec = pl.BlockSpec((None, D, tn), lambda i, j: (layer, 0, j))
    return pl.pallas_call(
        _ffn_in_kernel,
        grid=(L // tm, F // tn),
        in_specs=[pl.BlockSpec((tm, D), lambda i, j: (i, 0)), wspec, wspec],
        out_specs=pl.BlockSpec((tm, tn), lambda i, j: (i, j)),
        out_shape=jax.ShapeDtypeStruct((L, F), BF16),
        compiler_params=_params("parallel", "arbitrary"),
        name="ffn_in",
    )(h, wg, wu)


def _out_proj_kernel(a_ref, b_ref, c_ref, ga_ref, gb_ref, w_ref, o_ref, h_ref):
    @pl.when(pl.program_id(1) == 0)
    def _():
        wa = a_ref.shape[1]
        wb = b_ref.shape[1]
        h_ref[:, :wa] = _rms(a_ref[...], ga_ref[...]).astype(h_ref.dtype)
        h_ref[:, wa:wa + wb] = _rms(b_ref[...], gb_ref[...]).astype(h_ref.dtype)
        h_ref[:, wa + wb:] = c_ref[...]

    o_ref[...] = _dot(h_ref[...], w_ref[...])


def _out_proj(a, b, c, ga, gb, w, layer):
    L = a.shape[0]
    K = a.shape[1] + b.shape[1] + c.shape[1]
    N = w.shape[-1]
    tm = _pick(L, (512, 256, 128))
    tn = _pick(N, (1024, 512, 256, 128))
    rows = lambda arr: pl.BlockSpec((tm, arr.shape[1]), lambda i, j: (i, 0))
    gain = lambda arr: pl.BlockSpec((1, arr.shape[1]), lambda i, j: (0, 0))
    return pl.pallas_call(
        _out_proj_kernel,
        grid=(L // tm, N // tn),
        in_specs=[rows(a), rows(b), rows(c), gain(ga), gain(gb),
                  pl.BlockSpec((None, K, tn), lambda i, j: (layer, 0, j))],
        out_specs=pl.BlockSpec((tm, tn), lambda i, j: (i, j)),
        out_shape=jax.ShapeDtypeStruct((L, N), F32),
        scratch_shapes=[pltpu.VMEM((tm, K), BF16)],
        compiler_params=_params("parallel", "arbitrary"),
        name="out_proj",
    )(a, b, c, ga, gb, w)


def _sb_kernel(q_ref, k_ref, v_ref, o_ref, *, tq, hp, scale):
    i = pl.program_id(1)
    heads = range(hp)
    cols = lambda h: slice(h * HEAD_DIM, (h + 1) * HEAD_DIM)
    qs = [q_ref[:, cols(h)] for h in heads]
    row = lax.broadcasted_iota(jnp.int32, (tq, tq), 0)
    col = lax.broadcasted_iota(jnp.int32, (tq, tq), 1)
    from_here = jnp.where(row >= col, 1.0, 0.0).astype(BF16)
    from_here = jnp.concatenate([from_here, from_here], axis=0)
    causal = col < row

    def blocks(off, carries, diagonal):
        zs = [_dot_nt(qs[h], k_ref[pl.ds(off, tq), cols(h)]) * scale for h in heads]
        sps, splits = [], []
        for z in zs:
            sp = jnp.maximum(z, 0.0) + jnp.log(1.0 + jnp.exp(-jnp.abs(z)))
            if diagonal:
                sp = jnp.where(causal, sp, 0.0)
            hi = sp.astype(BF16)
            lo = (sp - hi.astype(F32)).astype(BF16)
            sps.append(sp)
            splits.append(jnp.concatenate([hi, lo], axis=1))
        suffixes = [_dot(s, from_here) for s in splits]
        pvs = []
        for h in heads:
            a = jnp.exp(zs[h] - suffixes[h] - carries[h])
            if diagonal:
                a = jnp.where(causal, a, 0.0)
            pvs.append(_dot(a.astype(BF16), v_ref[pl.ds(off, tq), cols(h)]))
        carries = [carries[h] + jnp.sum(sps[h], axis=-1, keepdims=True) for h in heads]
        return pvs, carries

    def live(carries):
        return jnp.min(jnp.broadcast_to(functools.reduce(jnp.minimum, carries), (tq, LANES)))

    accs, carries = blocks(pl.multiple_of(i * tq, tq), [jnp.zeros((tq, 1), F32)] * hp, True)

    def cond(st):
        return jnp.logical_and(st[0] < i, st[1] < SB_UNDERFLOW)

    def body(st):
        s, _, accs, carries = st
        pvs, carries = blocks(pl.multiple_of((i - 1 - s) * tq, tq), carries, False)
        return s + 1, live(carries), [a + p for a, p in zip(accs, pvs)], carries

    _, _, accs, _ = lax.while_loop(cond, body, (jnp.int32(0), live(carries), accs, carries))
    for h in heads:
        o_ref[:, cols(h)] = accs[h]


def _sb_attention(qkv, n_heads):
    L = qkv.shape[0]
    tq = _pick(L, (256, 128))
    hp = _pick(n_heads, (SB_HEADS_PER_STEP, 1))
    w = hp * HEAD_DIM
    nb = n_heads // hp
    kern = functools.partial(_sb_kernel, tq=tq, hp=hp, scale=HEAD_DIM ** -0.5)
    return pl.pallas_call(
        kern,
        grid=(nb, L // tq),
        in_specs=[
            pl.BlockSpec((tq, w), lambda h, i: (i, h)),
            pl.BlockSpec((L, w), lambda h, i: (0, nb + h)),
            pl.BlockSpec((L, w), lambda h, i: (0, 2 * nb + h)),
        ],
        out_specs=pl.BlockSpec((tq, w), lambda h, i: (i, h)),
        out_shape=jax.ShapeDtypeStruct((L, n_heads * HEAD_DIM), F32),
        compiler_params=_params("parallel", "arbitrary"),
        name="sb_attention",
    )(qkv, qkv, qkv)


def _dil_kernel(q_ref, k_ref, v_ref, o_ref, qd, kd, vd, ob, mb, sb, *, scale):
    S = DIL_SUPER
    B = DIL_BLOCK
    n = pl.program_id(1)
    first = n == 0
    cur = pl.multiple_of((n % 2) * S, S)
    old = pl.multiple_of(((n + 1) % 2) * S, S)

    @pl.when(first)
    def _():
        kd[...] = jnp.zeros_like(kd)
        vd[...] = jnp.zeros_like(vd)

    for bi, (window, d) in enumerate(DIL_PATTERNS):
        per = S // d
        for r in range(d):
            src = pl.ds(r, per, stride=d) if d > 1 else pl.ds(0, per)
            qd[bi, r * per:(r + 1) * per, :] = q_ref[src, :].astype(BF16)
            kd[bi, pl.ds(cur + r * per, per), :] = k_ref[src, :].astype(BF16)
            vd[bi, pl.ds(cur + r * per, per), :] = v_ref[src, :].astype(BF16)

    qi = lax.broadcasted_iota(jnp.int32, (B, 2 * B), 0)
    ki = lax.broadcasted_iota(jnp.int32, (B, 2 * B), 1)
    upto_self = ki <= qi + B
    band = jnp.logical_and(ki >= qi, upto_self)
    band_first = jnp.logical_and(ki >= jnp.maximum(qi, jnp.where(first, B, 0)), upto_self)

    def window_of(ref, bi, r, blk, per, nblk):
        if blk > 0:
            return ref[bi, pl.ds(cur + r * per + (blk - 1) * B, 2 * B), :]
        return jnp.concatenate([ref[bi, pl.ds(old + r * per + (nblk - 1) * B, B), :],
                                ref[bi, pl.ds(cur + r * per, B), :]], axis=0)

    for bi, (window, d) in enumerate(DIL_PATTERNS):
        per = S // d
        nblk = per // B
        for r in range(d):
            for blk in range(nblk):
                q = qd[bi, r * per + blk * B:r * per + (blk + 1) * B, :]
                sc = _dot_nt(q, window_of(kd, bi, r, blk, per, nblk)) * scale
                sc = jnp.where(band_first if blk == 0 else band, sc, -jnp.inf)
                m = jnp.max(sc, axis=-1, keepdims=True)
                p = jnp.exp(sc - m)
                s = jnp.sum(p, axis=-1, keepdims=True)
                dst = pl.ds(blk * B * d + r, B, stride=d) if d > 1 else pl.ds(blk * B, B)
                ob[bi, dst, :] = _dot(p.astype(BF16), window_of(vd, bi, r, blk, per, nblk))
                mb[bi, dst, :] = jnp.broadcast_to(m, (B, LANES))
                sb[bi, dst, :] = jnp.broadcast_to(s, (B, LANES))

    n_br = len(DIL_PATTERNS)
    chunk = 256

    def combine(c, _):
        r = pl.ds(pl.multiple_of(c * chunk, chunk), chunk)
        ms = [mb[b, r, :] for b in range(n_br)]
        m_max = functools.reduce(jnp.maximum, ms)
        ws = [jnp.exp(m - m_max) for m in ms]
        num = sum(ws[b] * ob[b, r, :] for b in range(n_br))
        den = sum(ws[b] * sb[b, r, :] for b in range(n_br))
        o_ref[r, :] = num / den
        return 0

    lax.fori_loop(0, S // chunk, combine, 0)


def _dil_attention(qkv, n_heads):
    L = qkv.shape[0]
    S = DIL_SUPER
    assert L % S == 0 and all(w // d == DIL_BLOCK and S % (DIL_BLOCK * d) == 0 for w, d in DIL_PATTERNS)
    block = lambda which: pl.BlockSpec((S, HEAD_DIM), lambda h, n: (n, which * n_heads + h))
    n_br = len(DIL_PATTERNS)
    return pl.pallas_call(
        functools.partial(_dil_kernel, scale=HEAD_DIM ** -0.5),
        grid=(n_heads, L // S),
        in_specs=[block(0), block(1), block(2)],
        out_specs=pl.BlockSpec((S, HEAD_DIM), lambda h, n: (n, h)),
        out_shape=jax.ShapeDtypeStruct((L, n_heads * HEAD_DIM), F32),
        scratch_shapes=[pltpu.VMEM((n_br, S, HEAD_DIM), BF16), pltpu.VMEM((n_br, 2 * S, HEAD_DIM), BF16),
                        pltpu.VMEM((n_br, 2 * S, HEAD_DIM), BF16),
                        pltpu.VMEM((n_br, S, HEAD_DIM), F32), pltpu.VMEM((n_br, S, LANES), F32),
                        pltpu.VMEM((n_br, S, LANES), F32)],
        compiler_params=_params("arbitrary", "arbitrary"),
        name="dilated_attention",
    )(qkv, qkv, qkv)


def _s5_prep_kernel(lre_ref, lim_ref, ldt_ref, bre_ref, bim_ref, pre_ref, pim_ref, bbre_ref, bbim_ref):
    a_re = jnp.minimum(lre_ref[...], -1e-4)
    a_im = lim_ref[...]
    dt = jnp.exp(ldt_ref[...])
    mag = jnp.exp(dt * a_re)
    ang = dt * a_im
    abar_re = mag * jnp.cos(ang)
    abar_im = mag * jnp.sin(ang)
    den = a_re * a_re + a_im * a_im
    nr = abar_re - 1.0
    f_re = (nr * a_re + abar_im * a_im) / den
    f_im = (abar_im * a_re - nr * a_im) / den
    for c in range(bre_ref.shape[0]):
        br = bre_ref[c]
        bi = bim_ref[c]
        bbre_ref[c] = f_re * br - f_im * bi
        bbim_ref[c] = f_re * bi + f_im * br
    pre_ref[0] = abar_re
    pim_ref[0] = abar_im

    def power(k, p):
        pr, pi = p
        pr, pi = pr * abar_re - pi * abar_im, pr * abar_im + pi * abar_re
        pre_ref[k] = pr
        pim_ref[k] = pi
        return pr, pi

    lax.fori_loop(1, pre_ref.shape[0], power, (abar_re, abar_im))


def _s5_scan_kernel(u_ref, wb_ref, wc_ref, pre_ref, pim_ref, d_ref, a_ref, w_ref, y_ref, o_ref,
                    up_ref, x_ref, yp_ref, carry_ref, *, n_scan, steps_per_tile):
    S = STATES_PER_TILE
    tc = u_ref.shape[0]
    J = tc // SUBLANES
    step = pl.program_id(0)

    def side_tile():
        o_ref[...] = _dot(a_ref[...], w_ref[...].astype(BF16)).astype(o_ref.dtype)

    @pl.when(jnp.logical_and(step < n_scan, step % steps_per_tile == 0))
    def _():
        carry_ref[...] = jnp.zeros_like(carry_ref)

    @pl.when(step >= n_scan)
    def _():
        side_tile()

    @pl.when(step < n_scan)
    def _():
        for s in range(SUBLANES):
            up_ref[pl.ds(s, J, stride=SUBLANES), :] = u_ref[pl.ds(s * J, J), :]
        x_ref[...] = _dot(up_ref[...].astype(BF16), wb_ref[...])
        side_tile()

        a_re = jnp.broadcast_to(pre_ref[0:1, :], (SUBLANES, S))
        a_im = jnp.broadcast_to(pim_ref[0:1, :], (SUBLANES, S))
        zero = jnp.zeros((SUBLANES, S), F32)
        e_re, e_im = zero, zero
        for j in range(J):
            r = slice(j * SUBLANES, (j + 1) * SUBLANES)
            e_re, e_im = (a_re * e_re - a_im * e_im + x_ref[r, :S],
                          a_re * e_im + a_im * e_re + x_ref[r, S:])
            x_ref[r, :S] = e_re
            x_ref[r, S:] = e_im

        aj_re = pre_ref[J - 1:J, :]
        aj_im = pim_ref[J - 1:J, :]
        g_re = carry_ref[0:1, :S]
        g_im = carry_ref[0:1, S:]
        sub = lax.broadcasted_iota(jnp.int32, (SUBLANES, S), 0)
        start_re = zero
        start_im = zero
        for s in range(SUBLANES):
            start_re = jnp.where(sub == s, g_re, start_re)
            start_im = jnp.where(sub == s, g_im, start_im)
            g_re, g_im = (e_re[s:s + 1, :] + aj_re * g_re - aj_im * g_im,
                          e_im[s:s + 1, :] + aj_re * g_im + aj_im * g_re)
        carry_ref[0:1, :S] = g_re
        carry_ref[0:1, S:] = g_im

        for j in range(J):
            r = slice(j * SUBLANES, (j + 1) * SUBLANES)
            p_re = pre_ref[j:j + 1, :]
            p_im = pim_ref[j:j + 1, :]
            x_ref[r, :S] = x_ref[r, :S] + p_re * start_re - p_im * start_im
            x_ref[r, S:] = x_ref[r, S:] + p_re * start_im + p_im * start_re

        yp_ref[...] = _dot(x_ref[...].astype(BF16), wc_ref[...])
        for s in range(SUBLANES):
            rows = pl.ds(s * J, J)
            y_ref[rows, :] = yp_ref[pl.ds(s, J, stride=SUBLANES), :] + d_ref[...] * u_ref[rows, :]


def _s5_glu_kernel(y_ref, w_ref, b_ref, gn_ref, o_ref):
    g = jax.nn.gelu(y_ref[...])
    out = g * jax.nn.sigmoid(_dot(g.astype(BF16), w_ref[...]) + b_ref[...])
    o_ref[...] = _rms(out, gn_ref[...]).astype(o_ref.dtype)


def _block_diag_tiles(m):
    G, a, b = m.shape
    T = G // GROUPS_PER_TILE
    m = m.reshape(T, GROUPS_PER_TILE, a, b)
    eye = jnp.eye(GROUPS_PER_TILE, dtype=m.dtype)
    return jnp.einsum("tgab,gh->tgahb", m, eye).reshape(T, GROUPS_PER_TILE * a, GROUPS_PER_TILE * b)


def _s5_layer(u_src, u_col0, lam_re, lam_im, log_dt, b_re, b_im, c_re, c_im, d_skip, w_glu, b_glu, g_out, layer,
              side_a, side_w, side_cols):
    L = u_src.shape[0]
    G, N = lam_re.shape
    C = G * SSM_GROUP
    T = G // GROUPS_PER_TILE
    S = STATES_PER_TILE
    assert N == SSM_STATE and C % LANES == 0 and u_col0 % LANES == 0
    ub = u_col0 // LANES
    tc = _pick(L, (1024, 512, 256, 128))
    n_pow = tc // SUBLANES

    full = lambda shape: pl.BlockSpec(shape, lambda: (0,) * len(shape))
    p_re, p_im, bb_re, bb_im = pl.pallas_call(
        _s5_prep_kernel,
        in_specs=[full((G, N)), full((G, N)), full((G, N)), full((SSM_GROUP, G, N)), full((SSM_GROUP, G, N))],
        out_specs=[full((n_pow, G, N)), full((n_pow, G, N)),
                   full((SSM_GROUP, G, N)), full((SSM_GROUP, G, N))],
        out_shape=[jax.ShapeDtypeStruct((n_pow, G, N), F32)] * 2
                  + [jax.ShapeDtypeStruct((SSM_GROUP, G, N), F32)] * 2,
        name="s5_prep",
    )(lam_re, lam_im, jnp.broadcast_to(log_dt[:, None], (G, N)),
      jnp.transpose(b_re, (2, 0, 1)), jnp.transpose(b_im, (2, 0, 1)))

    bb = lambda t: _block_diag_tiles(jnp.transpose(t, (1, 0, 2)))
    wb = jnp.concatenate([bb(bb_re), bb(bb_im)], axis=-1).astype(BF16)
    cc = lambda t: _block_diag_tiles(jnp.transpose(t, (0, 2, 1)))
    wc = jnp.concatenate([cc(c_re), -cc(c_im)], axis=1).astype(BF16)
    pw = lambda t: jnp.transpose(t.reshape(n_pow, T, S), (1, 0, 2))

    nt = L // tc
    n_scan = T * nt
    K = side_a.shape[1]
    side_tn = _pick(side_cols, (512, 256, 128))
    nj = side_cols // side_tn
    n_side = (L // tc) * nj
    assert side_a.shape[0] == L and n_side >= n_scan
    ct = lambda s: jnp.minimum(s, n_scan - 1) // nt
    tt = lambda s: jnp.minimum(s, n_scan - 1) % nt
    y, side = pl.pallas_call(
        functools.partial(_s5_scan_kernel, n_scan=n_scan, steps_per_tile=nt),
        grid=(n_side,),
        in_specs=[
            pl.BlockSpec((tc, LANES), lambda s: (tt(s), ub + ct(s))),
            pl.BlockSpec((None, LANES, 2 * S), lambda s: (ct(s), 0, 0)),
            pl.BlockSpec((None, 2 * S, LANES), lambda s: (ct(s), 0, 0)),
            pl.BlockSpec((None, n_pow, S), lambda s: (ct(s), 0, 0)),
            pl.BlockSpec((None, n_pow, S), lambda s: (ct(s), 0, 0)),
            pl.BlockSpec((1, LANES), lambda s: (0, ct(s))),
            pl.BlockSpec((tc, K), lambda s: (s // nj, 0)),
            pl.BlockSpec((None, K, side_tn), lambda s: (layer, 0, s % nj)),
        ],
        out_specs=[pl.BlockSpec((tc, LANES), lambda s: (tt(s), ct(s))),
                   pl.BlockSpec((tc, side_tn), lambda s: (s // nj, s % nj))],
        out_shape=[jax.ShapeDtypeStruct((L, C), F32), jax.ShapeDtypeStruct((L, side_cols), BF16)],
        scratch_shapes=[pltpu.VMEM((tc, LANES), F32), pltpu.VMEM((tc, 2 * S), F32),
                        pltpu.VMEM((tc, LANES), F32), pltpu.VMEM((SUBLANES, 2 * S), F32)],
        compiler_params=_params("arbitrary"),
        name="s5_scan",
    )(u_src, wb, wc, pw(p_re), pw(p_im), d_skip, side_a, side_w)

    tm = _pick(L, (512, 256, 128))
    return pl.pallas_call(
        _s5_glu_kernel,
        grid=(L // tm,),
        in_specs=[pl.BlockSpec((tm, C), lambda i: (i, 0)),
                  pl.BlockSpec((None, C, C), lambda i: (layer, 0, 0)),
                  pl.BlockSpec((1, C), lambda i: (0, 0)),
                  pl.BlockSpec((1, C), lambda i: (0, 0))],
        out_specs=pl.BlockSpec((tm, C), lambda i: (i, 0)),
        out_shape=jax.ShapeDtypeStruct((L, C), BF16),
        compiler_params=_params("parallel"),
        name="s5_glu",
    )(y, w_glu, b_glu, g_out), side


def kernel(x, norm_mix_pre, norm_mix_post, norm_ffn_pre, norm_ffn_post, w_in, norm_out_sb, norm_out_dil, norm_out_ssm, ssm_lambda_re, ssm_lambda_im, ssm_log_dt, ssm_b_re, ssm_b_im, ssm_c_re, ssm_c_im, ssm_d, ssm_w_glu, ssm_b_glu, w_out, ffn_w_gate, ffn_w_up, ffn_w_down):
    B, L, D = x.shape
    depth = w_in.shape[0]
    ssm_c = ssm_d.shape[-1]
    sb_w = norm_out_sb.shape[-1]
    dl_w = norm_out_dil.shape[-1]
    n_sb = sb_w // HEAD_DIM
    n_dl = dl_w // HEAD_DIM
    assert w_in.shape[-1] == 3 * sb_w + 3 * dl_w + ssm_c
    big = (1024, 512, 256, 128)

    w_out_b = w_out.astype(BF16)
    w_glu_b = ssm_w_glu.astype(BF16)
    row = lambda a, l: a[l][None, :]

    outs = []
    for b in range(B):
        xb = x[b]
        h = _norm(xb, row(norm_mix_pre, 0))
        for l in range(depth):
            rest = _mm(h, w_in, l, 3 * sb_w, 3 * dl_w + ssm_c, F32, big, (512, 256, 128))
            o_ssm, qkv_sb = _s5_layer(rest, 3 * dl_w, ssm_lambda_re[l], ssm_lambda_im[l], ssm_log_dt[l],
                                      ssm_b_re[l], ssm_b_im[l], ssm_c_re[l], ssm_c_im[l], row(ssm_d, l),
                                      w_glu_b, row(ssm_b_glu, l), row(norm_out_ssm, l), l, h, w_in, 3 * sb_w)
            o_sb = _sb_attention(qkv_sb, n_sb)
            o_dl = _dil_attention(rest, n_dl)
            y = _out_proj(o_sb, o_dl, o_ssm, row(norm_out_sb, l), row(norm_out_dil, l), w_out_b, l)
            xb, h = _resid_norm(xb, y, row(norm_mix_post, l), row(norm_ffn_pre, l))
            f = _ffn_in(h, ffn_w_gate, ffn_w_up, l)
            y = _mm(f, ffn_w_down, l, 0, D, F32, big, (256, 128), single_buffer_a=True)
            if l + 1 < depth:
                xb, h = _resid_norm(xb, y, row(norm_ffn_post, l), row(norm_mix_pre, l + 1))
            else:
                xb = _resid_norm(xb, y, row(norm_ffn_post, l))
        outs.append(xb)
    return jnp.stack(outs)
```
